```python
import jax, jax.numpy as jnp
from jax import lax
import numpy as np

D_MODEL = 2048
BATCH = 4
SEQ = 2048
DEPTH = 2

CONV_WIDTH = D_MODEL // 2
CONV_KERNEL = 31
POOL_WIDTH = D_MODEL // 2
POOL_WINDOWS = (2, 4, 8, 16)
N_POOL_GROUPS = len(POOL_WINDOWS)
POOL_GROUP_IN = POOL_WIDTH // N_POOL_GROUPS
POOL_GROUP_OUT = D_MODEL // N_POOL_GROUPS
IN_COLS = 2 * CONV_WIDTH + POOL_WIDTH + 2 * D_MODEL
N_GROUPS = 4
EXPERTS_PER_GROUP = 8
TOP_K_IN_GROUP = 2
D_EXPERT = D_MODEL // 8
PLE_DIM = 256
EPS = 1e-6

kernel_name = "hybrid_conv_pool_hmoe_ple"


def rms_norm(x, g):
    xf = x.astype(jnp.float32)
    y = xf * lax.rsqrt(jnp.mean(xf * xf, axis=-1, keepdims=True) + EPS)
    return (y * g.astype(jnp.float32)).astype(x.dtype)


def layer_norm(x, g, b):
    xf = x.astype(jnp.float32)
    mu = jnp.mean(xf, axis=-1, keepdims=True)
    xc = xf - mu
    var = jnp.mean(xc * xc, axis=-1, keepdims=True)
    y = xc * lax.rsqrt(var + EPS)
    return (y * g.astype(jnp.float32) + b.astype(jnp.float32)).astype(x.dtype)


def conformer_conv_branch(a_in, b_glu, conv_w, conv_b, ln_g, ln_b, w_conv_out):
    a = a_in + b_glu
    a = a[..., :CONV_WIDTH] * jax.nn.sigmoid(a[..., CONV_WIDTH:])
    a = lax.conv_general_dilated(
        a, conv_w[:, None, :].astype(a.dtype), window_strides=(1,),
        padding=[(CONV_KERNEL - 1, 0)], dimension_numbers=("NWC", "WIO", "NWC"),
        feature_group_count=CONV_WIDTH) + conv_b
    a = jax.nn.silu(layer_norm(a, ln_g, ln_b))
    return a @ w_conv_out


def multiscale_pool_branch(v, pool_w, pool_scale):
    S = v.shape[1]
    vf = v.astype(jnp.float32)
    csum = jnp.cumsum(vf, axis=1)
    pos = jnp.arange(S) + 1
    outs = []
    for g, w in enumerate(POOL_WINDOWS):
        sl = slice(g * POOL_GROUP_IN, (g + 1) * POOL_GROUP_IN)
        c = csum[..., sl]
        prev = jnp.pad(c, ((0, 0), (w, 0), (0, 0)))[:, :S]
        count = jnp.minimum(pos, w).astype(jnp.float32)[None, :, None]
        outs.append((c - prev) / count - vf[..., sl])
    pooled = jnp.stack(outs, axis=2).astype(v.dtype)
    y = jnp.einsum("bsgc,gcd->bsgd", pooled, pool_w)
    return y.reshape(v.shape[0], S, D_MODEL) * pool_scale


def hierarchical_moe(v, rg_w, rg_b, re_w, re_b, w_gate, w_up, w_down):
    B, S, D = v.shape
    vt = v.reshape(B * S, D)
    g_logits = (vt @ rg_w + rg_b).astype(jnp.float32)
    g_probs = jax.nn.softmax(g_logits, axis=-1)
    g_val, g_idx = lax.top_k(g_probs, 1)
    e_logits = (vt @ re_w + re_b).astype(jnp.float32).reshape(-1, N_GROUPS, EXPERTS_PER_GROUP)
    e_sel = jnp.take_along_axis(e_logits, g_idx[:, :, None], axis=1)[:, 0]
    e_probs = jax.nn.softmax(e_sel, axis=-1)
    e_val, e_idx = lax.top_k(e_probs, TOP_K_IN_GROUP)
    e_val = e_val / jnp.sum(e_val, axis=-1, keepdims=True)
    weights = g_val * e_val
    combine = jnp.einsum("ng,nke,nk->nge",
                         jax.nn.one_hot(g_idx[:, 0], N_GROUPS, dtype=jnp.float32),
                         jax.nn.one_hot(e_idx, EXPERTS_PER_GROUP, dtype=jnp.float32),
                         weights).astype(v.dtype)
    y = jnp.zeros_like(vt)
    for g in range(N_GROUPS):
        h = jax.nn.silu(jnp.einsum("nd,edf->nef", vt, w_gate[g])) * jnp.einsum("nd,edf->nef", vt, w_up[g])
        h = h * combine[:, g, :, None]
        y = y + jnp.einsum("nef,efd->nd", h, w_down[g])
    return y.reshape(B, S, D)


def setup_inputs(seed: int = 0) -> dict:
    key = jax.random.key(seed)
    ks = jax.random.split(key, 24)
    f32 = jnp.float32
    L, D, Cc, E, G, F = DEPTH, D_MODEL, CONV_WIDTH, EXPERTS_PER_GROUP, N_GROUPS, D_EXPERT

    def nrm(k, shape, scale):
        return jax.random.normal(k, shape, f32) * scale

    def gain(k, shape):
        return 1.0 + 0.02 * jax.random.normal(k, shape, f32)

    return {
        "x": nrm(ks[0], (BATCH, SEQ, D), 1.0),
        "p": nrm(ks[1], (DEPTH, BATCH, SEQ, PLE_DIM), 1.0),
        "norm_mix_g": gain(ks[2], (L, D)),
        "w_in": nrm(ks[3], (L, D, IN_COLS), D ** -0.5),
        "b_glu": nrm(ks[4], (L, 2 * Cc), 0.02),
        "conv_w": nrm(ks[5], (L, CONV_KERNEL, Cc), CONV_KERNEL ** -0.5),
        "conv_b": nrm(ks[6], (L, Cc), 0.02),
        "conv_ln_g": gain(ks[7], (L, Cc)),
        "conv_ln_b": nrm(ks[8], (L, Cc), 0.02),
        "w_conv_out": nrm(ks[9], (L, Cc, D), Cc ** -0.5),
        "pool_w": nrm(ks[10], (L, N_POOL_GROUPS, POOL_GROUP_IN, POOL_GROUP_OUT), POOL_GROUP_IN ** -0.5),
        "pool_scale": gain(ks[11], (L, D)),
        "w_out": nrm(ks[12], (L, D, D), D ** -0.5),
        "norm_ffn_g": gain(ks[13], (L, D)),
        "router_group_w": nrm(ks[14], (L, D, G), D ** -0.5),
        "router_group_b": nrm(ks[15], (L, G), 0.01),
        "router_expert_w": nrm(ks[16], (L, D, G * E), D ** -0.5),
        "router_expert_b": nrm(ks[17], (L, G * E), 0.01),
        "expert_w_gate": nrm(ks[18], (L, G, E, D, F), D ** -0.5),
        "expert_w_up": nrm(ks[19], (L, G, E, D, F), D ** -0.5),
        "expert_w_down": nrm(ks[20], (L, G, E, F, D), F ** -0.5),
        "norm_ple_g": gain(ks[21], (L, D)),
        "ple_gate_w": nrm(ks[22], (L, D, D), D ** -0.5),
        "ple_proj_w": nrm(ks[23], (L, PLE_DIM, D), PLE_DIM ** -0.5),
        "final_norm_g": gain(jax.random.fold_in(key, 99), (D,)),
    }


def reference(x, p, norm_mix_g, w_in, b_glu, conv_w, conv_b, conv_ln_g, conv_ln_b, w_conv_out,
              pool_w, pool_scale, w_out, norm_ffn_g, router_group_w, router_group_b,
              router_expert_w, router_expert_b, expert_w_gate, expert_w_up, expert_w_down,
              norm_ple_g, ple_gate_w, ple_proj_w, final_norm_g):
    s0 = 2 * CONV_WIDTH
    s1 = s0 + POOL_WIDTH
    s2 = s1 + D_MODEL
    for i in range(DEPTH):
        u = rms_norm(x, norm_mix_g[i])
        z = u @ w_in[i]
        a_in, pool_in, gate_a, gate_b = z[..., :s0], z[..., s0:s1], z[..., s1:s2], z[..., s2:]
        branch_a = conformer_conv_branch(a_in, b_glu[i], conv_w[i], conv_b[i],
                                         conv_ln_g[i], conv_ln_b[i], w_conv_out[i])
        branch_b = multiscale_pool_branch(pool_in, pool_w[i], pool_scale[i])
        merged = jax.nn.sigmoid(gate_a) * branch_a + jax.nn.sigmoid(gate_b) * branch_b
        x = x + merged @ w_out[i]
        v = rms_norm(x, norm_ffn_g[i])
        x = x + hierarchical_moe(v, router_group_w[i], router_group_b[i], router_expert_w[i],
                                 router_expert_b[i], expert_w_gate[i], expert_w_up[i], expert_w_down[i])
        ple = p[i] @ ple_proj_w[i]
        gate = jax.nn.sigmoid(rms_norm(x, norm_ple_g[i]) @ ple_gate_w[i])
        x = x + gate * ple
    return rms_norm(x, final_norm_g)
```

```python
import functools

import jax
import jax.numpy as jnp
from jax import lax
from jax.experimental import pallas as pl
from jax.experimental.pallas import tpu as pltpu

F32 = jnp.float32
BF16 = jnp.bfloat16

D_MODEL = 2048
CONV_WIDTH = 1024
CONV_KERNEL = 31
POOL_WIDTH = 1024
POOL_WINDOWS = (2, 4, 8, 16)
POOL_GROUP_IN = 256
POOL_GROUP_OUT = 512
N_GROUPS = 4
EXPERTS_PER_GROUP = 8
N_EXPERTS = N_GROUPS * EXPERTS_PER_GROUP
D_EXPERT = 256
EPS = 1e-6

SUBLANES = 8
LANES = 128
VMEM_LIMIT = 56 * 1024 * 1024

TM_IN = 512
TN_IN = 1024
TN_GLU = 512
TM_MIX = 256
TM_PLE = 256
TM_MOE = 256
CONV_HALO = 32
POOL_HALO = 16
CONV_ROWS = 64
ROUTER_LANES = 128


def _rms(x, g):
    return x * lax.rsqrt(jnp.mean(x * x, axis=-1, keepdims=True) + EPS) * g


def _sigmoid(x):
    return 1.0 / (1.0 + jnp.exp(-x))


def _dot(a, b):
    return jnp.dot(a, b, preferred_element_type=F32)


def _params(sem):
    return pltpu.CompilerParams(dimension_semantics=sem, vmem_limit_bytes=VMEM_LIMIT)


def _norm_kernel(x_ref, g_ref, o_ref):
    o_ref[...] = _rms(x_ref[...], g_ref[...]).astype(o_ref.dtype)


def _norm_call(x, g):
    n = x.shape[0]
    tm = 512
    return pl.pallas_call(
        _norm_kernel,
        grid=(n // tm,),
        in_specs=[pl.BlockSpec((tm, D_MODEL), lambda i: (i, 0)),
                  pl.BlockSpec((1, D_MODEL), lambda i: (0, 0))],
        out_specs=pl.BlockSpec((tm, D_MODEL), lambda i: (i, 0)),
        out_shape=jax.ShapeDtypeStruct((n, D_MODEL), BF16),
        compiler_params=_params(("arbitrary",)),
        name="norm",
    )(x, g)


def _glu_kernel(u_ref, w1_ref, w2_ref, b1_ref, b2_ref, o_ref, w1s, w2s):
    @pl.when(pl.program_id(1) == 0)
    def _():
        w1s[...] = w1_ref[...].astype(BF16)
        w2s[...] = w2_ref[...].astype(BF16)

    u = u_ref[...]
    a = _dot(u, w1s[...]) + b1_ref[...]
    g = _dot(u, w2s[...]) + b2_ref[...]
    o_ref[...] = (a * _sigmoid(g)).astype(o_ref.dtype)


def _glu_call(u, w_in_l, b_glu_l):
    n = u.shape[0]
    nj = CONV_WIDTH // TN_GLU
    return pl.pallas_call(
        _glu_kernel,
        grid=(nj, n // TM_IN),
        in_specs=[pl.BlockSpec((TM_IN, D_MODEL), lambda j, i: (i, 0)),
                  pl.BlockSpec((D_MODEL, TN_GLU), lambda j, i: (0, j)),
                  pl.BlockSpec((D_MODEL, TN_GLU), lambda j, i: (0, j + nj)),
                  pl.BlockSpec((1, TN_GLU), lambda j, i: (0, j)),
                  pl.BlockSpec((1, TN_GLU), lambda j, i: (0, j + nj))],
        out_specs=pl.BlockSpec((TM_IN, TN_GLU), lambda j, i: (i, j)),
        out_shape=jax.ShapeDtypeStruct((n, CONV_WIDTH), BF16),
        scratch_shapes=[pltpu.VMEM((D_MODEL, TN_GLU), BF16), pltpu.VMEM((D_MODEL, TN_GLU), BF16)],
        compiler_params=_params(("arbitrary", "arbitrary")),
        name="glu",
    )(u, w_in_l, w_in_l, b_glu_l, b_glu_l)


def _z_kernel(u_ref, w_ref, o_ref, ws):
    j = pl.program_id(0)

    @pl.when(pl.program_id(1) == 0)
    def _():
        ws[...] = w_ref[...].astype(BF16)

    z = _dot(u_ref[...], ws[...])

    @pl.when(j == 0)
    def _():
        o_ref[...] = z.astype(o_ref.dtype)

    @pl.when(j > 0)
    def _():
        o_ref[...] = _sigmoid(z).astype(o_ref.dtype)


def _z_call(u, w_in_l):
    n = u.shape[0]
    width = POOL_WIDTH + 2 * D_MODEL
    col0 = (2 * CONV_WIDTH) // TN_IN
    return pl.pallas_call(
        _z_kernel,
        grid=(width // TN_IN, n // TM_IN),
        in_specs=[pl.BlockSpec((TM_IN, D_MODEL), lambda j, i: (i, 0)),
                  pl.BlockSpec((D_MODEL, TN_IN), lambda j, i: (0, j + col0))],
        out_specs=pl.BlockSpec((TM_IN, TN_IN), lambda j, i: (i, j)),
        out_shape=jax.ShapeDtypeStruct((n, width), BF16),
        scratch_shapes=[pltpu.VMEM((D_MODEL, TN_IN), BF16)],
        compiler_params=_params(("arbitrary", "arbitrary")),
        name="zproj",
    )(u, w_in_l)


def _conv_taps(r):
    lead = CONV_HALO - (CONV_KERNEL - 1)
    return [(q, SUBLANES * q + r - lead) for q in range((lead + CONV_KERNEL) // SUBLANES + 1)
            if 0 <= SUBLANES * q + r - lead < CONV_KERNEL]


def _mix_kernel(tiles_per_seq,
                x_ref, glu_ref, z_ref, cw_ref, cb_ref, lng_ref, lnb_ref, wco_ref, pw_ref,
                ps_ref, wo_ref, ng_ref, rwh_ref, rwl_ref, rb_ref,
                x1_ref, v_ref, eid_ref, rwt_ref,
                cbuf, pbuf, conv_s):
    tm = TM_MIX
    seq_tile = pl.program_id(0) % tiles_per_seq

    @pl.when(seq_tile == 0)
    def _():
        cbuf[0:CONV_HALO, :] = jnp.zeros((CONV_HALO, CONV_WIDTH), F32)
        pbuf[0:POOL_HALO, :] = jnp.zeros((POOL_HALO, POOL_WIDTH), F32)

    cbuf[CONV_HALO:CONV_HALO + tm, :] = glu_ref[...].astype(F32)
    pbuf[POOL_HALO:POOL_HALO + tm, :] = z_ref[:, 0:POOL_WIDTH].astype(F32)

    for c in range(CONV_WIDTH // LANES):
        ls = slice(c * LANES, (c + 1) * LANES)
        for rb in range(tm // CONV_ROWS):
            acc = None
            for r in range(SUBLANES):
                taps = _conv_taps(r)
                qmin, qmax = taps[0][0], taps[-1][0]
                rows = CONV_ROWS + SUBLANES * (qmax - qmin)
                a = cbuf[pl.ds(rb * CONV_ROWS + SUBLANES * qmin + r, rows), ls]
                for q, k in taps:
                    o = SUBLANES * (q - qmin)
                    term = a[o:o + CONV_ROWS, :] * cw_ref[k:k + 1, ls]
                    acc = term if acc is None else acc + term
            conv_s[rb * CONV_ROWS:(rb + 1) * CONV_ROWS, ls] = acc + cb_ref[:, ls]
    cbuf[0:CONV_HALO, :] = cbuf[tm:tm + CONV_HALO, :]

    a = conv_s[...]
    mu = jnp.mean(a, axis=-1, keepdims=True)
    ac = a - mu
    var = jnp.mean(ac * ac, axis=-1, keepdims=True)
    a = ac * lax.rsqrt(var + EPS) * lng_ref[...] + lnb_ref[...]
    a = a * _sigmoid(a)
    branch_a = _dot(a.astype(BF16), wco_ref[...])

    pos1 = seq_tile * tm + lax.broadcasted_iota(jnp.int32, (tm, 1), 0) + 1
    merged = []
    for g, w in enumerate(POOL_WINDOWS):
        ls = slice(g * POOL_GROUP_IN, (g + 1) * POOL_GROUP_IN)
        cur = pbuf[pl.ds(POOL_HALO, tm), ls]
        acc = cur
        for d in range(1, w):
            acc = acc + pbuf[pl.ds(POOL_HALO - d, tm), ls]
        cnt = jnp.minimum(pos1, w).astype(F32)
        pooled = acc / cnt - cur
        os_ = slice(g * POOL_GROUP_OUT, (g + 1) * POOL_GROUP_OUT)
        yb = _dot(pooled.astype(BF16), pw_ref[g]) * ps_ref[:, os_]
        ga = z_ref[:, POOL_WIDTH + g * POOL_GROUP_OUT:POOL_WIDTH + (g + 1) * POOL_GROUP_OUT]
        gb = z_ref[:, POOL_WIDTH + D_MODEL + g * POOL_GROUP_OUT:
                   POOL_WIDTH + D_MODEL + (g + 1) * POOL_GROUP_OUT]
        m = ga.astype(F32) * branch_a[:, os_] + gb.astype(F32) * yb
        merged.append(m.astype(BF16))
    pbuf[0:POOL_HALO, :] = pbuf[tm:tm + POOL_HALO, :]
    merged = jnp.concatenate(merged, axis=-1)

    x1 = x_ref[...] + _dot(merged, wo_ref[...])
    x1_ref[...] = x1

    v = _rms(x1, ng_ref[...])
    v_ref[...] = v
    v_hi = v.astype(BF16)
    v_lo = (v - v_hi.astype(F32)).astype(BF16)
    logits = (_dot(v_hi, rwh_ref[...]) + _dot(v_lo, rwh_ref[...]) + _dot(v_hi, rwl_ref[...])
              + rb_ref[...])
    lane = lax.broadcasted_iota(jnp.int32, (tm, ROUTER_LANES), 1)
    lane_f = lane.astype(F32)
    neg = jnp.float32(-jnp.inf)
    big = jnp.float32(1e9)

    is_group = jnp.logical_and(lane >= N_EXPERTS, lane < N_EXPERTS + N_GROUPS)
    gl = jnp.where(is_group, logits, neg)
    gmax = jnp.max(gl, axis=-1, keepdims=True)
    gsum = jnp.sum(jnp.exp(gl - gmax), axis=-1, keepdims=True)
    gidx = jnp.min(jnp.where(gl == gmax, lane_f, big), axis=-1, keepdims=True) - N_EXPERTS
    g_val = 1.0 / gsum

    lane_group = jnp.right_shift(lane, 3).astype(F32)
    in_group = jnp.logical_and(lane < N_EXPERTS, lane_group == gidx)
    el = jnp.where(in_group, logits, neg)
    m1 = jnp.max(el, axis=-1, keepdims=True)
    i1 = jnp.min(jnp.where(el == m1, lane_f, big), axis=-1, keepdims=True)
    el2 = jnp.where(lane_f == i1, neg, el)
    m2 = jnp.max(el2, axis=-1, keepdims=True)
    i2 = jnp.min(jnp.where(el2 == m2, lane_f, big), axis=-1, keepdims=True)
    t = jnp.exp(m2 - m1)
    e1 = 1.0 / (1.0 + t)
    e2 = t / (1.0 + t)
    eid_ref[...] = jnp.where(lane == 0, i1, jnp.where(lane == 1, i2, 0.0)).astype(jnp.int32)
    rwt_ref[...] = jnp.where(lane == 0, g_val * e1, jnp.where(lane == 1, g_val * e2, 0.0))


def _mix_call(x, glu, z, cw, cb, lng, lnb, wco, pw, ps, wo, ng, rwh, rwl, rb, seq_len):
    n = x.shape[0]
    tm = TM_MIX
    zw = z.shape[1]
    const = lambda *shape: pl.BlockSpec(shape, lambda i: (0,) * len(shape))
    row = lambda width: pl.BlockSpec((tm, width), lambda i: (i, 0))
    return pl.pallas_call(
        functools.partial(_mix_kernel, seq_len // tm),
        grid=(n // tm,),
        in_specs=[row(D_MODEL), row(CONV_WIDTH), row(zw),
                  const(CONV_KERNEL, CONV_WIDTH), const(1, CONV_WIDTH), const(1, CONV_WIDTH),
                  const(1, CONV_WIDTH), const(CONV_WIDTH, D_MODEL),
                  const(len(POOL_WINDOWS), POOL_GROUP_IN, POOL_GROUP_OUT), const(1, D_MODEL),
                  const(D_MODEL, D_MODEL), const(1, D_MODEL),
                  const(D_MODEL, ROUTER_LANES), const(D_MODEL, ROUTER_LANES), const(1, ROUTER_LANES)],
        out_specs=[row(D_MODEL), row(D_MODEL), row(ROUTER_LANES), row(ROUTER_LANES)],
        out_shape=[jax.ShapeDtypeStruct((n, D_MODEL), F32), jax.ShapeDtypeStruct((n, D_MODEL), F32),
                   jax.ShapeDtypeStruct((n, ROUTER_LANES), jnp.int32),
                   jax.ShapeDtypeStruct((n, ROUTER_LANES), F32)],
        scratch_shapes=[pltpu.VMEM((CONV_HALO + tm, CONV_WIDTH), F32),
                        pltpu.VMEM((POOL_HALO + tm, POOL_WIDTH), F32),
                        pltpu.VMEM((tm, CONV_WIDTH), F32)],
        compiler_params=_params(("arbitrary",)),
        name="mix",
    )(x, glu, z, cw, cb, lng, lnb, wco, pw, ps, wo, ng, rwh, rwl, rb)


def _moe_kernel(te_ref, nv_ref, rtok_ref, rdst_ref,
                v_hbm, wg_ref, wu_ref, wd_ref, rw_ref, out_hbm,
                xbuf, ybuf, wgu_s, wd_s, gsem, ssem):
    tm = TM_MOE
    i = pl.program_id(0)
    nv = nv_ref[0]
    slot = i % 2

    def gather_row(tile, s, r):
        tok = rtok_ref[tile * tm + r]
        return pltpu.make_async_copy(v_hbm.at[pl.ds(tok, 1)], xbuf.at[s, pl.ds(r, 1)], gsem.at[s])

    def scatter_row(tile, s, r):
        dst = rdst_ref[tile * tm + r]
        return pltpu.make_async_copy(ybuf.at[s, pl.ds(r, 1)], out_hbm.at[pl.ds(dst, 1)], ssem.at[s])

    def start_rows(make, tile, s):
        def body(r, c):
            make(tile, s, r).start()
            return c
        lax.fori_loop(0, tm, body, 0, unroll=8)

    def wait_gather(s):
        pltpu.make_async_copy(v_hbm.at[pl.ds(0, tm)], xbuf.at[s], gsem.at[s]).wait()

    def wait_scatter(s):
        pltpu.make_async_copy(ybuf.at[s], out_hbm.at[pl.ds(0, tm)], ssem.at[s]).wait()

    @pl.when(i == 0)
    def _():
        start_rows(gather_row, 0, 0)
        n_real = out_hbm.shape[0] - 2 * tm
        ybuf[0] = jnp.zeros((tm, D_MODEL), F32)
        fills = [pltpu.make_async_copy(ybuf.at[0], out_hbm.at[pl.ds(n_real + s * tm, tm)], ssem.at[s])
                 for s in range(2)]
        for f in fills:
            f.start()
        for f in fills:
            f.wait()

    @pl.when(i < nv)
    def _():
        wait_gather(slot)

        @pl.when(i + 1 < nv)
        def _():
            start_rows(gather_row, i + 1, 1 - slot)

        changed = jnp.logical_or(i == 0, te_ref[i] != te_ref[jnp.maximum(i - 1, 0)])

        @pl.when(changed)
        def _():
            wgu_s[:, 0:D_EXPERT] = wg_ref[0].astype(BF16)
            wgu_s[:, D_EXPERT:2 * D_EXPERT] = wu_ref[0].astype(BF16)
            wd_s[...] = wd_ref[0].astype(BF16)

        xb = xbuf[slot].astype(BF16)
        gu = _dot(xb, wgu_s[...])
        gate = gu[:, 0:D_EXPERT]
        h = gate * _sigmoid(gate) * gu[:, D_EXPERT:2 * D_EXPERT] * rw_ref[...]
        y = _dot(h.astype(BF16), wd_s[...])

        @pl.when(i >= 2)
        def _():
            wait_scatter(slot)

        ybuf[slot] = y
        start_rows(scatter_row, i, slot)

        @pl.when(i == nv - 1)
        def _():
            wait_scatter(slot)

            @pl.when(i >= 1)
            def _():
                wait_scatter(1 - slot)


def _moe_call(v, wg, wu, wd, layer, tile_expert, n_valid, row_token, row_dst, row_w):
    n = v.shape[0]
    tm = TM_MOE
    n_tiles = row_token.shape[0] // tm
    widx = lambda i, te, nv, rt, rd: (layer * N_EXPERTS + te[i], 0, 0)
    return pl.pallas_call(
        _moe_kernel,
        grid_spec=pltpu.PrefetchScalarGridSpec(
            num_scalar_prefetch=4,
            grid=(n_tiles,),
            in_specs=[pl.BlockSpec(memory_space=pl.ANY),
                      pl.BlockSpec((1, D_MODEL, D_EXPERT), widx),
                      pl.BlockSpec((1, D_MODEL, D_EXPERT), widx),
                      pl.BlockSpec((1, D_EXPERT, D_MODEL), widx),
                      pl.BlockSpec((tm, 1), lambda i, te, nv, rt, rd: (i, 0))],
            out_specs=pl.BlockSpec(memory_space=pl.ANY),
            scratch_shapes=[pltpu.VMEM((2, tm, D_MODEL), F32), pltpu.VMEM((2, tm, D_MODEL), F32),
                            pltpu.VMEM((D_MODEL, 2 * D_EXPERT), BF16),
                            pltpu.VMEM((D_EXPERT, D_MODEL), BF16),
                            pltpu.SemaphoreType.DMA((2,)), pltpu.SemaphoreType.DMA((2,))],
        ),
        out_shape=jax.ShapeDtypeStruct((2 * n + 2 * tm, D_MODEL), F32),
        compiler_params=pltpu.CompilerParams(dimension_semantics=("arbitrary",),
                                             vmem_limit_bytes=VMEM_LIMIT),
        name="moe",
    )(tile_expert, n_valid, row_token, row_dst, v, wg, wu, wd, row_w)


def _dispatch_plan(eid, rwt, n):
    tm = TM_MOE
    n_rows = 2 * n + N_EXPERTS * tm
    flat_e = eid.reshape(-1)
    onehot = (flat_e[:, None] == jnp.arange(N_EXPERTS, dtype=jnp.int32)[None, :]).astype(jnp.int32)
    csum = jnp.cumsum(onehot, axis=0)
    rank = jnp.take_along_axis(csum, flat_e[:, None], axis=1)[:, 0] - 1
    counts = csum[-1]
    padded = ((counts + tm - 1) // tm) * tm
    ends = jnp.cumsum(padded)
    offs = ends - padded
    dest = offs[flat_e] + rank
    pair = jnp.arange(2 * n, dtype=jnp.int32)
    tok = pair // 2
    slot = pair % 2
    rows = jnp.arange(n_rows, dtype=jnp.int32)
    row_token = jnp.zeros((n_rows,), jnp.int32).at[dest].set(tok)
    row_dst = (2 * n + rows % (2 * tm)).at[dest].set(slot * n + tok)
    row_w = jnp.zeros((n_rows,), F32).at[dest].set(rwt.reshape(-1))
    n_valid = (ends[-1] // tm).astype(jnp.int32)
    starts = jnp.arange(n_rows // tm, dtype=jnp.int32) * tm
    te = jnp.searchsorted(ends, starts, side="right").astype(jnp.int32)
    te_last = te[jnp.maximum(n_valid - 1, 0)]
    te = jnp.where(starts // tm < n_valid, te, te_last)
    return te, n_valid.reshape(1), row_token, row_dst, row_w.reshape(-1, 1)


def _ple_kernel(last, x1_ref, ya_ref, yb_ref, p_ref, ng_ref, wg_ref, wp_ref, gn_ref, *out_refs):
    x2 = x1_ref[...] + ya_ref[...] + yb_ref[...]
    h = _rms(x2, ng_ref[...]).astype(BF16)
    gate = _sigmoid(_dot(h, wg_ref[...]))
    ple = _dot(p_ref[...].astype(BF16), wp_ref[...])
    x3 = x2 + gate * ple
    if last:
        out_refs[0][...] = _rms(x3, gn_ref[...])
    else:
        out_refs[0][...] = x3
        out_refs[1][...] = _rms(x3, gn_ref[...]).astype(BF16)


def _ple_call(x1, y2, p_l, ng, wg, wp, gn, last):
    n = x1.shape[0]
    tm = TM_PLE
    const = lambda *shape: pl.BlockSpec(shape, lambda i: (0,) * len(shape))
    row = lambda width: pl.BlockSpec((tm, width), lambda i: (i, 0))
    if last:
        out_specs = [row(D_MODEL)]
        out_shape = [jax.ShapeDtypeStruct((n, D_MODEL), F32)]
    else:
        out_specs = [row(D_MODEL), row(D_MODEL)]
        out_shape = [jax.ShapeDtypeStruct((n, D_MODEL), F32), jax.ShapeDtypeStruct((n, D_MODEL), BF16)]
    return pl.pallas_call(
        functools.partial(_ple_kernel, last),
        grid=(n // tm,),
        in_specs=[row(D_MODEL), row(D_MODEL),
                  pl.BlockSpec((tm, D_MODEL), lambda i: (i + n // tm, 0)),
                  row(p_l.shape[1]), const(1, D_MODEL), const(D_MODEL, D_MODEL),
                  const(p_l.shape[1], D_MODEL), const(1, D_MODEL)],
        out_specs=out_specs,
        out_shape=out_shape,
        compiler_params=_params(("arbitrary",)),
        name="ple",
    )(x1, y2, y2, p_l, ng, wg, wp, gn)


def kernel(x, p, norm_mix_g, w_in, b_glu, conv_w, conv_b, conv_ln_g, conv_ln_b, w_conv_out, pool_w, pool_scale, w_out, norm_ffn_g, router_group_w, router_group_b, router_expert_w, router_expert_b, expert_w_gate, expert_w_up, expert_w_down, norm_ple_g, ple_gate_w, ple_proj_w, final_norm_g):
    batch, seq, d = x.shape
    depth = w_in.shape[0]
    n = batch * seq
    assert d == D_MODEL and seq % TM_MIX == 0 and n % TM_IN == 0

    xf = x.reshape(n, d)
    pf = p.reshape(depth, n, p.shape[-1])
    row2 = lambda a: a.reshape(1, -1)
    wg_all = expert_w_gate.reshape(depth * N_EXPERTS, D_MODEL, D_EXPERT)
    wu_all = expert_w_up.reshape(depth * N_EXPERTS, D_MODEL, D_EXPERT)
    wd_all = expert_w_down.reshape(depth * N_EXPERTS, D_EXPERT, D_MODEL)

    u = _norm_call(xf, row2(norm_mix_g[0]))
    out = None
    for l in range(depth):
        glu = _glu_call(u, w_in[l], row2(b_glu[l]))
        z = _z_call(u, w_in[l])

        pad = ROUTER_LANES - N_EXPERTS - N_GROUPS
        rw = jnp.concatenate([router_expert_w[l], router_group_w[l], jnp.zeros((d, pad), F32)], axis=1)
        rb = jnp.concatenate([router_expert_b[l], router_group_b[l], jnp.zeros((pad,), F32)])
        rw_hi = rw.astype(BF16)
        rw_lo = (rw - rw_hi.astype(F32)).astype(BF16)
        x1, v, eid, rwt = _mix_call(
            xf, glu, z, conv_w[l], row2(conv_b[l]), row2(conv_ln_g[l]), row2(conv_ln_b[l]),
            w_conv_out[l].astype(BF16), pool_w[l].astype(BF16), row2(pool_scale[l]),
            w_out[l].astype(BF16), row2(norm_ffn_g[l]), rw_hi, rw_lo, row2(rb), seq)

        plan = _dispatch_plan(eid[:, :2], rwt[:, :2], n)
        y2 = _moe_call(v, wg_all, wu_all, wd_all, l, *plan)

        last = l == depth - 1
        gn = final_norm_g if last else norm_mix_g[l + 1]
        res = _ple_call(x1, y2, pf[l], row2(norm_ple_g[l]), ple_gate_w[l].astype(BF16),
                        ple_proj_w[l].astype(BF16), row2(gn), last)
        if last:
            out = res[0]
        else:
            xf, u = res
    return out.reshape(batch, seq, d)
```

```python
import functools

import jax
import jax.numpy as jnp
from jax import lax
from jax.experimental import pallas as pl
from jax.experimental.pallas import tpu as pltpu

F32 = jnp.float32
BF16 = jnp.bfloat16
I32 = jnp.int32

D_MODEL = 2048
CONV_WIDTH = 1024
CONV_KERNEL = 31
POOL_WIDTH = 1024
POOL_WINDOWS = (2, 4, 8, 16)
POOL_GROUP_IN = 256
POOL_GROUP_OUT = 512
N_GROUPS = 4
EXPERTS_PER_GROUP = 8
N_EXPERTS = N_GROUPS * EXPERTS_PER_GROUP
D_EXPERT = 256
EPS = 1e-6

SUBLANES = 8
LANES = 128
CHUNKS = D_MODEL // LANES
VMEM_LIMIT = 56 * 1024 * 1024

TM_IN = 512
TN_IN = 1024
TN_GLU = 512
TM_MIX = 256
TM_PLAN = 512
TM_PLE = 256
TM_MOE = 256
CONV_HALO = 32
POOL_HALO = 16
CONV_ROWS = 64
ROUTER_LANES = 128


def _rms(x, g):
    return x * lax.rsqrt(jnp.mean(x * x, axis=-1, keepdims=True) + EPS) * g


def _sigmoid(x):
    return 1.0 / (1.0 + jnp.exp(-x))


def _dot(a, b):
    return jnp.dot(a, b, preferred_element_type=F32)


def _params(sem):
    return pltpu.CompilerParams(dimension_semantics=sem, vmem_limit_bytes=VMEM_LIMIT)


def _to_chunked(ref, x):
    rows = x.shape[0]
    for c in range(CHUNKS):
        ref[pl.ds(c, rows, stride=CHUNKS), :] = x[:, c * LANES:(c + 1) * LANES]


def _from_chunked(ref, rows):
    return jnp.concatenate([ref[pl.ds(c, rows, stride=CHUNKS), :] for c in range(CHUNKS)], axis=-1)


def _norm_kernel(x_ref, g_ref, o_ref):
    o_ref[...] = _rms(x_ref[...], g_ref[...]).astype(o_ref.dtype)


def _norm_call(x, g):
    n = x.shape[0]
    tm = 512
    return pl.pallas_call(
        _norm_kernel,
        grid=(n // tm,),
        in_specs=[pl.BlockSpec((tm, D_MODEL), lambda i: (i, 0)),
                  pl.BlockSpec((1, D_MODEL), lambda i: (0, 0))],
        out_specs=pl.BlockSpec((tm, D_MODEL), lambda i: (i, 0)),
        out_shape=jax.ShapeDtypeStruct((n, D_MODEL), BF16),
        compiler_params=_params(("arbitrary",)),
        name="norm",
    )(x, g)


def _glu_kernel(u_ref, w1_ref, w2_ref, b1_ref, b2_ref, o_ref, w1s, w2s):
    @pl.when(pl.program_id(1) == 0)
    def _():
        w1s[...] = w1_ref[...].astype(BF16)
        w2s[...] = w2_ref[...].astype(BF16)

    u = u_ref[...]
    a = _dot(u, w1s[...]) + b1_ref[...]
    g = _dot(u, w2s[...]) + b2_ref[...]
    o_ref[...] = (a * _sigmoid(g)).astype(o_ref.dtype)


def _glu_call(u, w_in, b_glu, layer):
    n = u.shape[0]
    nj = CONV_WIDTH // TN_GLU
    return pl.pallas_call(
        _glu_kernel,
        grid=(nj, n // TM_IN),
        in_specs=[pl.BlockSpec((TM_IN, D_MODEL), lambda j, i: (i, 0)),
                  pl.BlockSpec((None, D_MODEL, TN_GLU), lambda j, i: (layer, 0, j)),
                  pl.BlockSpec((None, D_MODEL, TN_GLU), lambda j, i: (layer, 0, j + nj)),
                  pl.BlockSpec((None, 1, TN_GLU), lambda j, i: (layer, 0, j)),
                  pl.BlockSpec((None, 1, TN_GLU), lambda j, i: (layer, 0, j + nj))],
        out_specs=pl.BlockSpec((TM_IN, TN_GLU), lambda j, i: (i, j)),
        out_shape=jax.ShapeDtypeStruct((n, CONV_WIDTH), BF16),
        scratch_shapes=[pltpu.VMEM((D_MODEL, TN_GLU), BF16), pltpu.VMEM((D_MODEL, TN_GLU), BF16)],
        compiler_params=_params(("arbitrary", "arbitrary")),
        name="glu",
    )(u, w_in, w_in, b_glu, b_glu)


def _z_kernel(u_ref, w_ref, o_ref, ws):
    j = pl.program_id(0)

    @pl.when(pl.program_id(1) == 0)
    def _():
        ws[...] = w_ref[...].astype(BF16)

    z = _dot(u_ref[...], ws[...])

    @pl.when(j == 0)
    def _():
        o_ref[...] = z.astype(o_ref.dtype)

    @pl.when(j > 0)
    def _():
        o_ref[...] = _sigmoid(z).astype(o_ref.dtype)


def _z_call(u, w_in, layer):
    n = u.shape[0]
    width = POOL_WIDTH + 2 * D_MODEL
    col0 = (2 * CONV_WIDTH) // TN_IN
    return pl.pallas_call(
        _z_kernel,
        grid=(width // TN_IN, n // TM_IN),
        in_specs=[pl.BlockSpec((TM_IN, D_MODEL), lambda j, i: (i, 0)),
                  pl.BlockSpec((None, D_MODEL, TN_IN), lambda j, i: (layer, 0, j + col0))],
        out_specs=pl.BlockSpec((TM_IN, TN_IN), lambda j, i: (i, j)),
        out_shape=jax.ShapeDtypeStruct((n, width), BF16),
        scratch_shapes=[pltpu.VMEM((D_MODEL, TN_IN), BF16)],
        compiler_params=_params(("arbitrary", "arbitrary")),
        name="zproj",
    )(u, w_in)


def _conv_taps(r):
    lead = CONV_HALO - (CONV_KERNEL - 1)
    return [(q, SUBLANES * q + r - lead) for q in range((lead + CONV_KERNEL) // SUBLANES + 1)
            if 0 <= SUBLANES * q + r - lead < CONV_KERNEL]


def _mix_kernel(tiles_per_seq,
                x_ref, glu_ref, z_ref, cw_ref, cb_ref, lng_ref, lnb_ref, wco_ref, pw_ref,
                ps_ref, wo_ref, ng_ref, rwh_ref, rwl_ref, rb_ref,
                x1_ref, v_ref, eid_ref, rwt_ref, cnt_ref,
                cbuf, pbuf, conv_s):
    tm = TM_MIX
    seq_tile = pl.program_id(0) % tiles_per_seq

    @pl.when(seq_tile == 0)
    def _():
        cbuf[0:CONV_HALO, :] = jnp.zeros((CONV_HALO, CONV_WIDTH), F32)
        pbuf[0:POOL_HALO, :] = jnp.zeros((POOL_HALO, POOL_WIDTH), F32)

    cbuf[CONV_HALO:CONV_HALO + tm, :] = glu_ref[...].astype(F32)
    pbuf[POOL_HALO:POOL_HALO + tm, :] = z_ref[:, 0:POOL_WIDTH].astype(F32)

    for c in range(CONV_WIDTH // LANES):
        ls = slice(c * LANES, (c + 1) * LANES)
        for rb in range(tm // CONV_ROWS):
            acc = None
            for r in range(SUBLANES):
                taps = _conv_taps(r)
                qmin, qmax = taps[0][0], taps[-1][0]
                rows = CONV_ROWS + SUBLANES * (qmax - qmin)
                a = cbuf[pl.ds(rb * CONV_ROWS + SUBLANES * qmin + r, rows), ls]
                for q, k in taps:
                    o = SUBLANES * (q - qmin)
                    term = a[o:o + CONV_ROWS, :] * cw_ref[k:k + 1, ls]
                    acc = term if acc is None else acc + term
            conv_s[rb * CONV_ROWS:(rb + 1) * CONV_ROWS, ls] = acc + cb_ref[:, ls]
    cbuf[0:CONV_HALO, :] = cbuf[tm:tm + CONV_HALO, :]

    a = conv_s[...]
    mu = jnp.mean(a, axis=-1, keepdims=True)
    ac = a - mu
    var = jnp.mean(ac * ac, axis=-1, keepdims=True)
    a = ac * lax.rsqrt(var + EPS) * lng_ref[...] + lnb_ref[...]
    a = a * _sigmoid(a)
    branch_a = _dot(a.astype(BF16), wco_ref[...])

    pos1 = seq_tile * tm + lax.broadcasted_iota(I32, (tm, 1), 0) + 1
    merged = []
    for g, w in enumerate(POOL_WINDOWS):
        ls = slice(g * POOL_GROUP_IN, (g + 1) * POOL_GROUP_IN)
        cur = pbuf[pl.ds(POOL_HALO, tm), ls]
        acc = cur
        for d in range(1, w):
            acc = acc + pbuf[pl.ds(POOL_HALO - d, tm), ls]
        cnt = jnp.minimum(pos1, w).astype(F32)
        pooled = acc / cnt - cur
        os_ = slice(g * POOL_GROUP_OUT, (g + 1) * POOL_GROUP_OUT)
        yb = _dot(pooled.astype(BF16), pw_ref[g]) * ps_ref[:, os_]
        ga = z_ref[:, POOL_WIDTH + g * POOL_GROUP_OUT:POOL_WIDTH + (g + 1) * POOL_GROUP_OUT]
        gb = z_ref[:, POOL_WIDTH + D_MODEL + g * POOL_GROUP_OUT:
                   POOL_WIDTH + D_MODEL + (g + 1) * POOL_GROUP_OUT]
        m = ga.astype(F32) * branch_a[:, os_] + gb.astype(F32) * yb
        merged.append(m.astype(BF16))
    pbuf[0:POOL_HALO, :] = pbuf[tm:tm + POOL_HALO, :]
    merged = jnp.concatenate(merged, axis=-1)

    x1 = x_ref[...] + _dot(merged, wo_ref[...])
    x1_ref[...] = x1

    v = _rms(x1, ng_ref[...])
    _to_chunked(v_ref, v)
    v_hi = v.astype(BF16)
    v_lo = (v - v_hi.astype(F32)).astype(BF16)
    logits = (_dot(v_hi, rwh_ref[...]) + _dot(v_lo, rwh_ref[...]) + _dot(v_hi, rwl_ref[...])
              + rb_ref[...])
    lane = lax.broadcasted_iota(I32, (tm, ROUTER_LANES), 1)
    lane_f = lane.astype(F32)
    neg = jnp.float32(-jnp.inf)
    big = jnp.float32(1e9)

    is_group = jnp.logical_and(lane >= N_EXPERTS, lane < N_EXPERTS + N_GROUPS)
    gl = jnp.where(is_group, logits, neg)
    gmax = jnp.max(gl, axis=-1, keepdims=True)
    gsum = jnp.sum(jnp.exp(gl - gmax), axis=-1, keepdims=True)
    gidx = jnp.min(jnp.where(gl == gmax, lane_f, big), axis=-1, keepdims=True) - N_EXPERTS
    g_val = 1.0 / gsum

    lane_group = jnp.right_shift(lane, 3).astype(F32)
    in_group = jnp.logical_and(lane < N_EXPERTS, lane_group == gidx)
    el = jnp.where(in_group, logits, neg)
    m1 = jnp.max(el, axis=-1, keepdims=True)
    i1 = jnp.min(jnp.where(el == m1, lane_f, big), axis=-1, keepdims=True)
    el2 = jnp.where(lane_f == i1, neg, el)
    m2 = jnp.max(el2, axis=-1, keepdims=True)
    i2 = jnp.min(jnp.where(el2 == m2, lane_f, big), axis=-1, keepdims=True)
    t = jnp.exp(m2 - m1)
    e1 = 1.0 / (1.0 + t)
    e2 = t / (1.0 + t)
    eid_ref[...] = jnp.where(lane == 0, i1, jnp.where(lane == 1, i2, 0.0)).astype(I32)
    rwt_ref[...] = jnp.where(lane == 0, g_val * e1, jnp.where(lane == 1, g_val * e2, 0.0))

    @pl.when(pl.program_id(0) == 0)
    def _():
        cnt_ref[...] = jnp.zeros(cnt_ref.shape, F32)

    picked = jnp.where(jnp.logical_or(lane_f == i1, lane_f == i2), 1.0, 0.0)
    cnt_ref[0:1, :] += jnp.sum(picked, axis=0, keepdims=True)


def _mix_call(x, glu, z, cw, cb, lng, lnb, wco, pw, ps, wo, ng, rwh, rwl, rb, seq_len):
    n = x.shape[0]
    tm = TM_MIX
    zw = z.shape[1]
    const = lambda *shape: pl.BlockSpec(shape, lambda i: (0,) * len(shape))
    row = lambda width: pl.BlockSpec((tm, width), lambda i: (i, 0))
    return pl.pallas_call(
        functools.partial(_mix_kernel, seq_len // tm),
        grid=(n // tm,),
        in_specs=[row(D_MODEL), row(CONV_WIDTH), row(zw),
                  const(CONV_KERNEL, CONV_WIDTH), const(1, CONV_WIDTH), const(1, CONV_WIDTH),
                  const(1, CONV_WIDTH), const(CONV_WIDTH, D_MODEL),
                  const(len(POOL_WINDOWS), POOL_GROUP_IN, POOL_GROUP_OUT), const(1, D_MODEL),
                  const(D_MODEL, D_MODEL), const(1, D_MODEL),
                  const(D_MODEL, ROUTER_LANES), const(D_MODEL, ROUTER_LANES), const(1, ROUTER_LANES)],
        out_specs=[row(D_MODEL), pl.BlockSpec((tm * CHUNKS, LANES), lambda i: (i, 0)),
                   row(ROUTER_LANES), row(ROUTER_LANES), const(SUBLANES, ROUTER_LANES)],
        out_shape=[jax.ShapeDtypeStruct((n, D_MODEL), F32),
                   jax.ShapeDtypeStruct((n * CHUNKS, LANES), F32),
                   jax.ShapeDtypeStruct((n, ROUTER_LANES), I32),
                   jax.ShapeDtypeStruct((n, ROUTER_LANES), F32),
                   jax.ShapeDtypeStruct((SUBLANES, ROUTER_LANES), F32)],
        scratch_shapes=[pltpu.VMEM((CONV_HALO + tm, CONV_WIDTH), F32),
                        pltpu.VMEM((POOL_HALO + tm, POOL_WIDTH), F32),
                        pltpu.VMEM((tm, CONV_WIDTH), F32)],
        compiler_params=_params(("arbitrary",)),
        name="mix",
    )(x, glu, z, cw, cb, lng, lnb, wco, pw, ps, wo, ng, rwh, rwl, rb)


def _plan_kernel(eid_ref, offs_ref, dest_ref, carry):
    tp = TM_PLAN

    @pl.when(pl.program_id(0) == 0)
    def _():
        carry[...] = jnp.zeros(carry.shape, F32)

    eid = eid_ref[...]
    lane = lax.broadcasted_iota(I32, (tp, ROUTER_LANES), 1)
    hit0 = lane == eid[:, 0:1]
    hit1 = lane == eid[:, 1:2]
    onehot = jnp.where(jnp.logical_or(hit0, hit1), 1.0, 0.0)
    r = lax.broadcasted_iota(I32, (tp, tp), 0)
    c = lax.broadcasted_iota(I32, (tp, tp), 1)
    tri = jnp.where(c < r, 1.0, 0.0).astype(BF16)
    before = _dot(tri, onehot.astype(BF16)) + carry[0:1, :] + offs_ref[...]
    d0 = jnp.sum(jnp.where(hit0, before, 0.0), axis=-1, keepdims=True)
    d1 = jnp.sum(jnp.where(hit1, before, 0.0), axis=-1, keepdims=True)
    dest_ref[...] = jnp.where(lane == 0, d0, jnp.where(lane == 1, d1, 0.0)).astype(I32)
    carry[0:1, :] += jnp.sum(onehot, axis=0, keepdims=True)


def _plan_call(eid, offs):
    n = eid.shape[0]
    tp = TM_PLAN
    return pl.pallas_call(
        _plan_kernel,
        grid=(n // tp,),
        in_specs=[pl.BlockSpec((tp, ROUTER_LANES), lambda i: (i, 0)),
                  pl.BlockSpec((1, ROUTER_LANES), lambda i: (0, 0))],
        out_specs=pl.BlockSpec((tp, ROUTER_LANES), lambda i: (i, 0)),
        out_shape=jax.ShapeDtypeStruct((n, ROUTER_LANES), I32),
        scratch_shapes=[pltpu.VMEM((SUBLANES, ROUTER_LANES), F32)],
        compiler_params=_params(("arbitrary",)),
        name="plan",
    )(eid, offs)


def _moe_kernel(layer, n_tok,
                dest_ref, cnt_ref,
                v_hbm, wg_hbm, wu_hbm, wd_hbm, out_hbm,
                xbuf, ybuf, xb_s, yacc, stg_g, stg_u, stg_d, wgu_s, wd_s,
                row_pair, offs_s, nxt_s, first_s, cur_s,
                gsem, ssem, wsem):
    tm = TM_MOE
    j = pl.program_id(0)
    n_tiles = pl.num_programs(0)
    slot = j % 2
    row0 = j * tm

    def gather_row(tile, s, r):
        tok = lax.shift_right_logical(row_pair[tile * tm + r], 1)
        return pltpu.make_async_copy(v_hbm.at[pl.ds(pl.multiple_of(tok * CHUNKS, CHUNKS), CHUNKS)],
                                     xbuf.at[s, pl.ds(pl.multiple_of(r * CHUNKS, CHUNKS), CHUNKS)],
                                     gsem.at[s])

    def scatter_row(tile, s, r):
        a = row_pair[tile * tm + r]
        dst = (a & 1) * n_tok + lax.shift_right_logical(a, 1)
        return pltpu.make_async_copy(ybuf.at[s, pl.ds(pl.multiple_of(r * CHUNKS, CHUNKS), CHUNKS)],
                                     out_hbm.at[pl.ds(pl.multiple_of(dst * CHUNKS, CHUNKS), CHUNKS)],
                                     ssem.at[s])

    def start_rows(make, tile, s):
        def body(r, c):
            make(tile, s, r).start()
            return c
        lax.fori_loop(0, tm, body, 0, unroll=8)

    def wait_gather(s):
        pltpu.make_async_copy(v_hbm.at[pl.ds(0, tm * CHUNKS)], xbuf.at[s], gsem.at[s]).wait()

    def wait_scatter(s):
        pltpu.make_async_copy(ybuf.at[s], out_hbm.at[pl.ds(0, tm * CHUNKS)], ssem.at[s]).wait()

    def weight_copies(e):
        w = layer * N_EXPERTS + e
        return [pltpu.make_async_copy(wg_hbm.at[w], stg_g, wsem.at[0]),
                pltpu.make_async_copy(wu_hbm.at[w], stg_u, wsem.at[1]),
                pltpu.make_async_copy(wd_hbm.at[w], stg_d, wsem.at[2])]

    @pl.when(j == 0)
    def _():
        def off_body(e, acc):
            offs_s[e] = acc
            return acc + cnt_ref[e]
        offs_s[N_EXPERTS] = lax.fori_loop(0, N_EXPERTS, off_body, 0)

        def nxt_body(k, nxt):
            e = N_EXPERTS - 1 - k
            nxt_s[e] = nxt
            return jnp.where(cnt_ref[e] > 0, e, nxt)
        first_expert = lax.fori_loop(0, N_EXPERTS, nxt_body, N_EXPERTS)

        def first_body(t, e):
            e = lax.while_loop(lambda e: offs_s[e + 1] <= t * tm, lambda e: e + 1, e)
            first_s[t] = e
            return e
        lax.fori_loop(0, n_tiles, first_body, 0)

        def inv_body(a, c):
            row_pair[dest_ref[a]] = a
            return c
        lax.fori_loop(0, 2 * n_tok, inv_body, 0, unroll=8)

        cur_s[0] = -1
        for cp in weight_copies(first_expert):
            cp.start()
        start_rows(gather_row, 0, 0)

    wait_gather(slot)

    @pl.when(j + 1 < n_tiles)
    def _():
        start_rows(gather_row, j + 1, 1 - slot)

    xb_s[...] = _from_chunked(xbuf.at[slot], tm).astype(BF16)
    rows = lax.broadcasted_iota(I32, (tm, 1), 0)

    def expert_step(e):
        @pl.when(cnt_ref[e] > 0)
        def _():
            @pl.when(cur_s[0] != e)
            def _():
                for cp in weight_copies(e):
                    cp.wait()
                wgu_s[:, 0:D_EXPERT] = stg_g[...].astype(BF16)
                wgu_s[:, D_EXPERT:2 * D_EXPERT] = stg_u[...].astype(BF16)
                wd_s[...] = stg_d[...].astype(BF16)
                cur_s[0] = e
                nxt = nxt_s[e]

                @pl.when(nxt < N_EXPERTS)
                def _():
                    for cp in weight_copies(nxt):
                        cp.start()

            lo = offs_s[e] - row0
            hi = offs_s[e + 1] - row0
            gu = _dot(xb_s[...], wgu_s[...])
            gate = gu[:, 0:D_EXPERT]
            h = gate * _sigmoid(gate) * gu[:, D_EXPERT:2 * D_EXPERT]
            h = jnp.where(jnp.logical_and(rows >= lo, rows < hi), h, 0.0)
            y = _dot(h.astype(BF16), wd_s[...])

            @pl.when(e == first_s[j])
            def _():
                yacc[...] = y

            @pl.when(e != first_s[j])
            def _():
                yacc[...] += y
        return e + 1

    lax.while_loop(lambda e: offs_s[e] < row0 + tm, expert_step, first_s[j])

    @pl.when(j >= 2)
    def _():
        wait_scatter(slot)

    _to_chunked(ybuf.at[slot], yacc[...])
    start_rows(scatter_row, j, slot)

    @pl.when(j == n_tiles - 1)
    def _():
        wait_scatter(slot)
        wait_scatter(1 - slot)


def _moe_call(v, wg, wu, wd, layer, dest, counts):
    n = v.shape[0] // CHUNKS
    tm = TM_MOE
    n_tiles = (2 * n) // tm
    any_spec = pl.BlockSpec(memory_space=pl.ANY)
    return pl.pallas_call(
        functools.partial(_moe_kernel, layer, n),
        grid_spec=pltpu.PrefetchScalarGridSpec(
            num_scalar_prefetch=2,
            grid=(n_tiles,),
            in_specs=[any_spec, any_spec, any_spec, any_spec],
            out_specs=any_spec,
            scratch_shapes=[pltpu.VMEM((2, tm * CHUNKS, LANES), F32),
                            pltpu.VMEM((2, tm * CHUNKS, LANES), F32),
                            pltpu.VMEM((tm, D_MODEL), BF16),
                            pltpu.VMEM((tm, D_MODEL), F32),
                            pltpu.VMEM((D_MODEL, D_EXPERT), F32),
                            pltpu.VMEM((D_MODEL, D_EXPERT), F32),
                            pltpu.VMEM((D_EXPERT, D_MODEL), F32),
                            pltpu.VMEM((D_MODEL, 2 * D_EXPERT), BF16),
                            pltpu.VMEM((D_EXPERT, D_MODEL), BF16),
                            pltpu.SMEM((2 * n,), I32),
                            pltpu.SMEM((N_EXPERTS + 1,), I32),
                            pltpu.SMEM((N_EXPERTS,), I32),
                            pltpu.SMEM((n_tiles,), I32),
                            pltpu.SMEM((1,), I32),
                            pltpu.SemaphoreType.DMA((2,)), pltpu.SemaphoreType.DMA((2,)),
                            pltpu.SemaphoreType.DMA((3,))],
        ),
        out_shape=jax.ShapeDtypeStruct((2 * n * CHUNKS, LANES), F32),
        compiler_params=_params(("arbitrary",)),
        name="moe",
    )(dest, counts, v, wg, wu, wd)


def _ple_kernel(last, x1_ref, ya_ref, yb_ref, rwt_ref, p_ref, ng_ref, wg_ref, wp_ref, gn_ref, *out_refs):
    tm = TM_PLE
    rwt = rwt_ref[...]
    x2 = (x1_ref[...] + rwt[:, 0:1] * _from_chunked(ya_ref, tm) + rwt[:, 1:2] * _from_chunked(yb_ref, tm))
    h = _rms(x2, ng_ref[...]).astype(BF16)
    gate = _sigmoid(_dot(h, wg_ref[...]))
    ple = _dot(p_ref[...].astype(BF16), wp_ref[...])
    x3 = x2 + gate * ple
    if last:
        out_refs[0][...] = _rms(x3, gn_ref[...])
    else:
        out_refs[0][...] = x3
        out_refs[1][...] = _rms(x3, gn_ref[...]).astype(BF16)


def _ple_call(x1, y2, rwt, p_l, ng, wg, wp, gn, last):
    n = x1.shape[0]
    tm = TM_PLE
    const = lambda *shape: pl.BlockSpec(shape, lambda i: (0,) * len(shape))
    row = lambda width: pl.BlockSpec((tm, width), lambda i: (i, 0))
    if last:
        out_specs = [row(D_MODEL)]
        out_shape = [jax.ShapeDtypeStruct((n, D_MODEL), F32)]
    else:
        out_specs = [row(D_MODEL), row(D_MODEL)]
        out_shape = [jax.ShapeDtypeStruct((n, D_MODEL), F32), jax.ShapeDtypeStruct((n, D_MODEL), BF16)]
    return pl.pallas_call(
        functools.partial(_ple_kernel, last),
        grid=(n // tm,),
        in_specs=[row(D_MODEL),
                  pl.BlockSpec((tm * CHUNKS, LANES), lambda i: (i, 0)),
                  pl.BlockSpec((tm * CHUNKS, LANES), lambda i: (i + n // tm, 0)),
                  row(ROUTER_LANES), row(p_l.shape[1]), const(1, D_MODEL), const(D_MODEL, D_MODEL),
                  const(p_l.shape[1], D_MODEL), const(1, D_MODEL)],
        out_specs=out_specs,
        out_shape=out_shape,
        compiler_params=_params(("arbitrary",)),
        name="ple",
    )(x1, y2, y2, rwt, p_l, ng, wg, wp, gn)


def kernel(x, p, norm_mix_g, w_in, b_glu, conv_w, conv_b, conv_ln_g, conv_ln_b, w_conv_out, pool_w, pool_scale, w_out, norm_ffn_g, router_group_w, router_group_b, router_expert_w, router_expert_b, expert_w_gate, expert_w_up, expert_w_down, norm_ple_g, ple_gate_w, ple_proj_w, final_norm_g):
    batch, seq, d = x.shape
    depth = w_in.shape[0]
    n = batch * seq
    assert d == D_MODEL and seq % TM_MIX == 0 and n % TM_IN == 0 and (2 * n) % TM_MOE == 0

    xf = x.reshape(n, d)
    pf = p.reshape(depth, n, p.shape[-1])
    row2 = lambda a: a.reshape(1, -1)
    b_glu3 = b_glu.reshape(depth, 1, -1)
    wg_all = expert_w_gate.reshape(depth * N_EXPERTS, D_MODEL, D_EXPERT)
    wu_all = expert_w_up.reshape(depth * N_EXPERTS, D_MODEL, D_EXPERT)
    wd_all = expert_w_down.reshape(depth * N_EXPERTS, D_EXPERT, D_MODEL)

    u = _norm_call(xf, row2(norm_mix_g[0]))
    out = None
    for l in range(depth):
        glu = _glu_call(u, w_in, b_glu3, l)
        z = _z_call(u, w_in, l)

        pad = ROUTER_LANES - N_EXPERTS - N_GROUPS
        rw = jnp.concatenate([router_expert_w[l], router_group_w[l], jnp.zeros((d, pad), F32)], axis=1)
        rb = jnp.concatenate([router_expert_b[l], router_group_b[l], jnp.zeros((pad,), F32)])
        rw_hi = rw.astype(BF16)
        rw_lo = (rw - rw_hi.astype(F32)).astype(BF16)
        x1, v, eid, rwt, cnt = _mix_call(
            xf, glu, z, conv_w[l], row2(conv_b[l]), row2(conv_ln_g[l]), row2(conv_ln_b[l]),
            w_conv_out[l].astype(BF16), pool_w[l].astype(BF16), row2(pool_scale[l]),
            w_out[l].astype(BF16), row2(norm_ffn_g[l]), rw_hi, rw_lo, row2(rb), seq)

        cnt_row = cnt[0:1, :]
        offs = jnp.cumsum(cnt_row, axis=1) - cnt_row
        dest = _plan_call(eid, offs)[:, :2].reshape(-1)
        y2 = _moe_call(v, wg_all, wu_all, wd_all, l, dest, cnt_row[0, :N_EXPERTS].astype(I32))

        last = l == depth - 1
        gn = final_norm_g if last else norm_mix_g[l + 1]
        res = _ple_call(x1, y2, rwt, pf[l], row2(norm_ple_g[l]), ple_gate_w[l].astype(BF16),
                        ple_proj_w[l].astype(BF16), row2(gn), last)
        if last:
            out = res[0]
        else:
            xf, u = res
    return out.reshape(batch, seq, d)
```

```python
import functools

import jax
import jax.numpy as jnp
from jax import lax
from jax.experimental import pallas as pl
from jax.experimental.pallas import tpu as pltpu

F32 = jnp.float32
BF16 = jnp.bfloat16
I32 = jnp.int32

D_MODEL = 2048
CONV_WIDTH = 1024
CONV_KERNEL = 31
POOL_WIDTH = 1024
POOL_WINDOWS = (2, 4, 8, 16)
POOL_GROUP_IN = 256
POOL_GROUP_OUT = 512
N_GROUPS = 4
EXPERTS_PER_GROUP = 8
N_EXPERTS = N_GROUPS * EXPERTS_PER_GROUP
D_EXPERT = 256
EPS = 1e-6

SUBLANES = 8
LANES = 128
CHUNKS = D_MODEL // LANES
ROW_PITCH = CHUNKS + 4
VMEM_LIMIT = 56 * 1024 * 1024

TM_IN = 512
TN_IN = 1024
TN_GLU = 512
TM_MIX = 256
TM_PLAN = 512
TM_PLE = 256
TM_MOE = 256
CONV_HALO = 32
POOL_HALO = 16
CONV_ROWS = 64
ROUTER_LANES = 128


def _rms(x, g):
    return x * lax.rsqrt(jnp.mean(x * x, axis=-1, keepdims=True) + EPS) * g


def _sigmoid(x):
    return 1.0 / (1.0 + jnp.exp(-x))


def _dot(a, b):
    return jnp.dot(a, b, preferred_element_type=F32)


def _params(sem):
    return pltpu.CompilerParams(dimension_semantics=sem, vmem_limit_bytes=VMEM_LIMIT)


def _to_chunked(ref, x):
    rows = x.shape[0]
    for c in range(CHUNKS):
        ref[pl.ds(c, rows, stride=ROW_PITCH), :] = x[:, c * LANES:(c + 1) * LANES]


def _zero_pads(ref, rows):
    for c in range(CHUNKS, ROW_PITCH):
        ref[pl.ds(c, rows, stride=ROW_PITCH), :] = jnp.zeros((rows, LANES), F32)


def _from_chunked(ref, rows):
    return jnp.concatenate([ref[pl.ds(c, rows, stride=ROW_PITCH), :] for c in range(CHUNKS)], axis=-1)


def _norm_kernel(x_ref, g_ref, o_ref):
    o_ref[...] = _rms(x_ref[...], g_ref[...]).astype(o_ref.dtype)


def _norm_call(x, g):
    n = x.shape[0]
    tm = 512
    return pl.pallas_call(
        _norm_kernel,
        grid=(n // tm,),
        in_specs=[pl.BlockSpec((tm, D_MODEL), lambda i: (i, 0)),
                  pl.BlockSpec((1, D_MODEL), lambda i: (0, 0))],
        out_specs=pl.BlockSpec((tm, D_MODEL), lambda i: (i, 0)),
        out_shape=jax.ShapeDtypeStruct((n, D_MODEL), BF16),
        compiler_params=_params(("arbitrary",)),
        name="norm",
    )(x, g)


def _glu_kernel(u_ref, w1_ref, w2_ref, b1_ref, b2_ref, o_ref, w1s, w2s):
    @pl.when(pl.program_id(1) == 0)
    def _():
        w1s[...] = w1_ref[...].astype(BF16)
        w2s[...] = w2_ref[...].astype(BF16)

    u = u_ref[...]
    a = _dot(u, w1s[...]) + b1_ref[...]
    g = _dot(u, w2s[...]) + b2_ref[...]
    o_ref[...] = (a * _sigmoid(g)).astype(o_ref.dtype)


def _glu_call(u, w_in, b_glu, layer):
    n = u.shape[0]
    nj = CONV_WIDTH // TN_GLU
    return pl.pallas_call(
        _glu_kernel,
        grid=(nj, n // TM_IN),
        in_specs=[pl.BlockSpec((TM_IN, D_MODEL), lambda j, i: (i, 0)),
                  pl.BlockSpec((None, D_MODEL, TN_GLU), lambda j, i: (layer, 0, j)),
                  pl.BlockSpec((None, D_MODEL, TN_GLU), lambda j, i: (layer, 0, j + nj)),
                  pl.BlockSpec((None, 1, TN_GLU), lambda j, i: (layer, 0, j)),
                  pl.BlockSpec((None, 1, TN_GLU), lambda j, i: (layer, 0, j + nj))],
        out_specs=pl.BlockSpec((TM_IN, TN_GLU), lambda j, i: (i, j)),
        out_shape=jax.ShapeDtypeStruct((n, CONV_WIDTH), BF16),
        scratch_shapes=[pltpu.VMEM((D_MODEL, TN_GLU), BF16), pltpu.VMEM((D_MODEL, TN_GLU), BF16)],
        compiler_params=_params(("arbitrary", "arbitrary")),
        name="glu",
    )(u, w_in, w_in, b_glu, b_glu)


def _z_kernel(u_ref, w_ref, o_ref, ws):
    j = pl.program_id(0)

    @pl.when(pl.program_id(1) == 0)
    def _():
        ws[...] = w_ref[...].astype(BF16)

    z = _dot(u_ref[...], ws[...])

    @pl.when(j == 0)
    def _():
        o_ref[...] = z.astype(o_ref.dtype)

    @pl.when(j > 0)
    def _():
        o_ref[...] = _sigmoid(z).astype(o_ref.dtype)


def _z_call(u, w_in, layer):
    n = u.shape[0]
    width = POOL_WIDTH + 2 * D_MODEL
    col0 = (2 * CONV_WIDTH) // TN_IN
    return pl.pallas_call(
        _z_kernel,
        grid=(width // TN_IN, n // TM_IN),
        in_specs=[pl.BlockSpec((TM_IN, D_MODEL), lambda j, i: (i, 0)),
                  pl.BlockSpec((None, D_MODEL, TN_IN), lambda j, i: (layer, 0, j + col0))],
        out_specs=pl.BlockSpec((TM_IN, TN_IN), lambda j, i: (i, j)),
        out_shape=jax.ShapeDtypeStruct((n, width), BF16),
        scratch_shapes=[pltpu.VMEM((D_MODEL, TN_IN), BF16)],
        compiler_params=_params(("arbitrary", "arbitrary")),
        name="zproj",
    )(u, w_in)


def _conv_taps(r):
    lead = CONV_HALO - (CONV_KERNEL - 1)
    return [(q, SUBLANES * q + r - lead) for q in range((lead + CONV_KERNEL) // SUBLANES + 1)
            if 0 <= SUBLANES * q + r - lead < CONV_KERNEL]


def _mix_kernel(tiles_per_seq,
                x_ref, glu_ref, z_ref, cw_ref, cb_ref, lng_ref, lnb_ref, wco_ref, pw_ref,
                ps_ref, wo_ref, ng_ref, rwh_ref, rwl_ref, rb_ref,
                x1_ref, v_ref, eid_ref, rwt_ref, cnt_ref,
                cbuf, pbuf, conv_s):
    tm = TM_MIX
    seq_tile = pl.program_id(0) % tiles_per_seq

    @pl.when(seq_tile == 0)
    def _():
        cbuf[:, 0:CONV_HALO, :] = jnp.zeros((CONV_WIDTH // LANES, CONV_HALO, LANES), F32)
        pbuf[:, 0:POOL_HALO, :] = jnp.zeros((POOL_WIDTH // LANES, POOL_HALO, LANES), F32)

    for c in range(CONV_WIDTH // LANES):
        cbuf[c, CONV_HALO:CONV_HALO + tm, :] = glu_ref[:, c * LANES:(c + 1) * LANES].astype(F32)
    for c in range(POOL_WIDTH // LANES):
        pbuf[c, POOL_HALO:POOL_HALO + tm, :] = z_ref[:, c * LANES:(c + 1) * LANES].astype(F32)

    for c in range(CONV_WIDTH // LANES):
        ls = slice(c * LANES, (c + 1) * LANES)
        for rb in range(tm // CONV_ROWS):
            acc = None
            for r in range(SUBLANES):
                taps = _conv_taps(r)
                qmin, qmax = taps[0][0], taps[-1][0]
                rows = CONV_ROWS + SUBLANES * (qmax - qmin)
                a = cbuf[c, pl.ds(rb * CONV_ROWS + SUBLANES * qmin + r, rows), :]
                for q, k in taps:
                    o = SUBLANES * (q - qmin)
                    term = a[o:o + CONV_ROWS, :] * cw_ref[k:k + 1, ls]
                    acc = term if acc is None else acc + term
            conv_s[rb * CONV_ROWS:(rb + 1) * CONV_ROWS, ls] = acc + cb_ref[:, ls]
    cbuf[:, 0:CONV_HALO, :] = cbuf[:, tm:tm + CONV_HALO, :]

    a = conv_s[...]
    mu = jnp.mean(a, axis=-1, keepdims=True)
    ac = a - mu
    var = jnp.mean(ac * ac, axis=-1, keepdims=True)
    a = ac * lax.rsqrt(var + EPS) * lng_ref[...] + lnb_ref[...]
    a = a * _sigmoid(a)
    branch_a = _dot(a.astype(BF16), wco_ref[...])

    pos1 = seq_tile * tm + lax.broadcasted_iota(I32, (tm, 1), 0) + 1
    merged = []
    for g, w in enumerate(POOL_WINDOWS):
        cnt = jnp.minimum(pos1, w).astype(F32)
        halves = []
        for s in range(g * POOL_GROUP_IN // LANES, (g + 1) * POOL_GROUP_IN // LANES):
            cur = pbuf[s, pl.ds(POOL_HALO, tm), :]
            acc = cur
            for d in range(1, w):
                acc = acc + pbuf[s, pl.ds(POOL_HALO - d, tm), :]
            halves.append(acc / cnt - cur)
        pooled = jnp.concatenate(halves, axis=-1)
        os_ = slice(g * POOL_GROUP_OUT, (g + 1) * POOL_GROUP_OUT)
        yb = _dot(pooled.astype(BF16), pw_ref[g]) * ps_ref[:, os_]
        ga = z_ref[:, POOL_WIDTH + g * POOL_GROUP_OUT:POOL_WIDTH + (g + 1) * POOL_GROUP_OUT]
        gb = z_ref[:, POOL_WIDTH + D_MODEL + g * POOL_GROUP_OUT:
                   POOL_WIDTH + D_MODEL + (g + 1) * POOL_GROUP_OUT]
        m = ga.astype(F32) * branch_a[:, os_] + gb.astype(F32) * yb
        merged.append(m.astype(BF16))
    pbuf[:, 0:POOL_HALO, :] = pbuf[:, tm:tm + POOL_HALO, :]
    merged = jnp.concatenate(merged, axis=-1)

    x1 = x_ref[...] + _dot(merged, wo_ref[...])
    x1_ref[...] = x1

    v = _rms(x1, ng_ref[...])
    _to_chunked(v_ref, v)
    _zero_pads(v_ref, tm)
    v_hi = v.astype(BF16)
    v_lo = (v - v_hi.astype(F32)).astype(BF16)
    logits = (_dot(v_hi, rwh_ref[...]) + _dot(v_lo, rwh_ref[...]) + _dot(v_hi, rwl_ref[...])
              + rb_ref[...])
    lane = lax.broadcasted_iota(I32, (tm, ROUTER_LANES), 1)
    lane_f = lane.astype(F32)
    neg = jnp.float32(-jnp.inf)
    big = jnp.float32(1e9)

    is_group = jnp.logical_and(lane >= N_EXPERTS, lane < N_EXPERTS + N_GROUPS)
    gl = jnp.where(is_group, logits, neg)
    gmax = jnp.max(gl, axis=-1, keepdims=True)
    gsum = jnp.sum(jnp.exp(gl - gmax), axis=-1, keepdims=True)
    gidx = jnp.min(jnp.where(gl == gmax, lane_f, big), axis=-1, keepdims=True) - N_EXPERTS
    g_val = 1.0 / gsum

    lane_group = jnp.right_shift(lane, 3).astype(F32)
    in_group = jnp.logical_and(lane < N_EXPERTS, lane_group == gidx)
    el = jnp.where(in_group, logits, neg)
    m1 = jnp.max(el, axis=-1, keepdims=True)
    i1 = jnp.min(jnp.where(el == m1, lane_f, big), axis=-1, keepdims=True)
    el2 = jnp.where(lane_f == i1, neg, el)
    m2 = jnp.max(el2, axis=-1, keepdims=True)
    i2 = jnp.min(jnp.where(el2 == m2, lane_f, big), axis=-1, keepdims=True)
    t = jnp.exp(m2 - m1)
    e1 = 1.0 / (1.0 + t)
    e2 = t / (1.0 + t)
    eid_ref[...] = jnp.where(lane == 0, i1, jnp.where(lane == 1, i2, 0.0)).astype(I32)
    rwt_ref[...] = jnp.where(lane == 0, g_val * e1, jnp.where(lane == 1, g_val * e2, 0.0))

    @pl.when(pl.program_id(0) == 0)
    def _():
        cnt_ref[...] = jnp.zeros(cnt_ref.shape, F32)

    picked = jnp.where(jnp.logical_or(lane_f == i1, lane_f == i2), 1.0, 0.0)
    cnt_ref[0:1, :] += jnp.sum(picked, axis=0, keepdims=True)


def _mix_call(x, glu, z, cw, cb, lng, lnb, wco, pw, ps, wo, ng, rwh, rwl, rb, seq_len):
    n = x.shape[0]
    tm = TM_MIX
    zw = z.shape[1]
    const = lambda *shape: pl.BlockSpec(shape, lambda i: (0,) * len(shape))
    row = lambda width: pl.BlockSpec((tm, width), lambda i: (i, 0))
    return pl.pallas_call(
        functools.partial(_mix_kernel, seq_len // tm),
        grid=(n // tm,),
        in_specs=[row(D_MODEL), row(CONV_WIDTH), row(zw),
                  const(CONV_KERNEL, CONV_WIDTH), const(1, CONV_WIDTH), const(1, CONV_WIDTH),
                  const(1, CONV_WIDTH), const(CONV_WIDTH, D_MODEL),
                  const(len(POOL_WINDOWS), POOL_GROUP_IN, POOL_GROUP_OUT), const(1, D_MODEL),
                  const(D_MODEL, D_MODEL), const(1, D_MODEL),
                  const(D_MODEL, ROUTER_LANES), const(D_MODEL, ROUTER_LANES), const(1, ROUTER_LANES)],
        out_specs=[row(D_MODEL), pl.BlockSpec((tm * ROW_PITCH, LANES), lambda i: (i, 0)),
                   row(ROUTER_LANES), row(ROUTER_LANES), const(SUBLANES, ROUTER_LANES)],
        out_shape=[jax.ShapeDtypeStruct((n, D_MODEL), F32),
                   jax.ShapeDtypeStruct((n * ROW_PITCH, LANES), F32),
                   jax.ShapeDtypeStruct((n, ROUTER_LANES), I32),
                   jax.ShapeDtypeStruct((n, ROUTER_LANES), F32),
                   jax.ShapeDtypeStruct((SUBLANES, ROUTER_LANES), F32)],
        scratch_shapes=[pltpu.VMEM((CONV_WIDTH // LANES, CONV_HALO + tm, LANES), F32),
                        pltpu.VMEM((POOL_WIDTH // LANES, POOL_HALO + tm, LANES), F32),
                        pltpu.VMEM((tm, CONV_WIDTH), F32)],
        compiler_params=_params(("arbitrary",)),
        name="mix",
    )(x, glu, z, cw, cb, lng, lnb, wco, pw, ps, wo, ng, rwh, rwl, rb)


def _plan_kernel(eid_ref, offs_ref, dest_ref, carry):
    tp = TM_PLAN

    @pl.when(pl.program_id(0) == 0)
    def _():
        carry[...] = jnp.zeros(carry.shape, F32)

    eid = eid_ref[...]
    lane = lax.broadcasted_iota(I32, (tp, ROUTER_LANES), 1)
    hit0 = lane == eid[:, 0:1]
    hit1 = lane == eid[:, 1:2]
    onehot = jnp.where(jnp.logical_or(hit0, hit1), 1.0, 0.0)
    r = lax.broadcasted_iota(I32, (tp, tp), 0)
    c = lax.broadcasted_iota(I32, (tp, tp), 1)
    tri = jnp.where(c < r, 1.0, 0.0).astype(BF16)
    before = _dot(tri, onehot.astype(BF16)) + carry[0:1, :] + offs_ref[...]
    d0 = jnp.sum(jnp.where(hit0, before, 0.0), axis=-1, keepdims=True)
    d1 = jnp.sum(jnp.where(hit1, before, 0.0), axis=-1, keepdims=True)
    dest_ref[...] = jnp.where(lane == 0, d0, jnp.where(lane == 1, d1, 0.0)).astype(I32)
    carry[0:1, :] += jnp.sum(onehot, axis=0, keepdims=True)


def _plan_call(eid, offs):
    n = eid.shape[0]
    tp = TM_PLAN
    return pl.pallas_call(
        _plan_kernel,
        grid=(n // tp,),
        in_specs=[pl.BlockSpec((tp, ROUTER_LANES), lambda i: (i, 0)),
                  pl.BlockSpec((1, ROUTER_LANES), lambda i: (0, 0))],
        out_specs=pl.BlockSpec((tp, ROUTER_LANES), lambda i: (i, 0)),
        out_shape=jax.ShapeDtypeStruct((n, ROUTER_LANES), I32),
        scratch_shapes=[pltpu.VMEM((SUBLANES, ROUTER_LANES), F32)],
        compiler_params=_params(("arbitrary",)),
        name="plan",
    )(eid, offs)


def _moe_kernel(layer, n_tok,
                dest_ref, cnt_ref,
                v_hbm, wg_hbm, wu_hbm, wd_hbm, out_hbm,
                xbuf, ybuf, xb_s, stg_g, stg_u, stg_d, wgu_s, wd_s,
                row_src, row_dst, offs_s, nxt_s, first_s, cur_s,
                gsem, ssem, wsem):
    tm = TM_MOE
    j = pl.program_id(0)
    n_tiles = pl.num_programs(0)
    slot = j % 2
    row0 = j * tm

    def gather_row(tile, s, r):
        return pltpu.make_async_copy(v_hbm.at[pl.ds(row_src[tile * tm + r], CHUNKS)],
                                     xbuf.at[s, pl.ds(r * ROW_PITCH, CHUNKS)], gsem.at[s])

    def scatter_row(tile, s, r):
        return pltpu.make_async_copy(ybuf.at[s, pl.ds(r * ROW_PITCH, ROW_PITCH)],
                                     out_hbm.at[pl.ds(row_dst[tile * tm + r], ROW_PITCH)], ssem.at[s])

    def start_rows(make, tile, s):
        def body(r, c):
            make(tile, s, r).start()
            return c
        lax.fori_loop(0, tm, body, 0, unroll=8)

    def wait_gather(s):
        pltpu.make_async_copy(v_hbm.at[pl.ds(0, tm * CHUNKS)], xbuf.at[s, pl.ds(0, tm * CHUNKS)],
                              gsem.at[s]).wait()

    def wait_scatter(s):
        pltpu.make_async_copy(ybuf.at[s], out_hbm.at[pl.ds(0, tm * ROW_PITCH)], ssem.at[s]).wait()

    def weight_copies(e):
        w = layer * N_EXPERTS + e
        return [pltpu.make_async_copy(wg_hbm.at[w], stg_g, wsem.at[0]),
                pltpu.make_async_copy(wu_hbm.at[w], stg_u, wsem.at[1]),
                pltpu.make_async_copy(wd_hbm.at[w], stg_d, wsem.at[2])]

    @pl.when(j == 0)
    def _():
        def off_body(e, acc):
            offs_s[e] = acc
            return acc + cnt_ref[e]
        offs_s[N_EXPERTS] = lax.fori_loop(0, N_EXPERTS, off_body, 0)

        def nxt_body(k, nxt):
            e = N_EXPERTS - 1 - k
            nxt_s[e] = nxt
            return jnp.where(cnt_ref[e] > 0, e, nxt)
        first_expert = lax.fori_loop(0, N_EXPERTS, nxt_body, N_EXPERTS)

        def first_body(t, e):
            e = lax.while_loop(lambda e: offs_s[e + 1] <= t * tm, lambda e: e + 1, e)
            first_s[t] = e
            return e
        lax.fori_loop(0, n_tiles, first_body, 0)

        def inv_body(a, c):
            tok = lax.shift_right_logical(a, 1)
            r = dest_ref[a]
            row_src[r] = tok * ROW_PITCH
            row_dst[r] = ((a & 1) * n_tok + tok) * ROW_PITCH
            return c
        lax.fori_loop(0, 2 * n_tok, inv_body, 0, unroll=8)

        for s in range(2):
            _zero_pads(ybuf.at[s], tm)
        cur_s[0] = -1
        for cp in weight_copies(first_expert):
            cp.start()
        start_rows(gather_row, 0, 0)

    wait_gather(slot)

    @pl.when(j + 1 < n_tiles)
    def _():
        start_rows(gather_row, j + 1, 1 - slot)

    xb_s[...] = _from_chunked(xbuf.at[slot], tm).astype(BF16)
    rows = lax.broadcasted_iota(I32, (tm, 1), 0)

    @pl.when(j >= 2)
    def _():
        wait_scatter(slot)

    def expert_step(e):
        @pl.when(cnt_ref[e] > 0)
        def _():
            @pl.when(cur_s[0] != e)
            def _():
                for cp in weight_copies(e):
                    cp.wait()
                wgu_s[:, 0:D_EXPERT] = stg_g[...].astype(BF16)
                wgu_s[:, D_EXPERT:2 * D_EXPERT] = stg_u[...].astype(BF16)
                wd_s[...] = stg_d[...].astype(BF16)
                cur_s[0] = e
                nxt = nxt_s[e]

                @pl.when(nxt < N_EXPERTS)
                def _():
                    for cp in weight_copies(nxt):
                        cp.start()

            lo = offs_s[e] - row0
            hi = offs_s[e + 1] - row0
            gu = _dot(xb_s[...], wgu_s[...])
            gate = gu[:, 0:D_EXPERT]
            h = gate * _sigmoid(gate) * gu[:, D_EXPERT:2 * D_EXPERT]
            h = jnp.where(jnp.logical_and(rows >= lo, rows < hi), h, 0.0)
            y = _dot(h.astype(BF16), wd_s[...])

            @pl.when(e == first_s[j])
            def _():
                _to_chunked(ybuf.at[slot], y)

            @pl.when(e != first_s[j])
            def _():
                _to_chunked(ybuf.at[slot], _from_chunked(ybuf.at[slot], tm) + y)
        return e + 1

    lax.while_loop(lambda e: offs_s[e] < row0 + tm, expert_step, first_s[j])

    start_rows(scatter_row, j, slot)

    @pl.when(j == n_tiles - 1)
    def _():
        wait_scatter(slot)
        wait_scatter(1 - slot)


def _moe_call(v, wg, wu, wd, layer, dest, counts):
    n = v.shape[0] // ROW_PITCH
    tm = TM_MOE
    n_tiles = (2 * n) // tm
    any_spec = pl.BlockSpec(memory_space=pl.ANY)
    return pl.pallas_call(
        functools.partial(_moe_kernel, layer, n),
        grid_spec=pltpu.PrefetchScalarGridSpec(
            num_scalar_prefetch=2,
            grid=(n_tiles,),
            in_specs=[any_spec, any_spec, any_spec, any_spec],
            out_specs=any_spec,
            scratch_shapes=[pltpu.VMEM((2, tm * ROW_PITCH, LANES), F32),
                            pltpu.VMEM((2, tm * ROW_PITCH, LANES), F32),
                            pltpu.VMEM((tm, D_MODEL), BF16),
                            pltpu.VMEM((D_MODEL, D_EXPERT), F32),
                            pltpu.VMEM((D_MODEL, D_EXPERT), F32),
                            pltpu.VMEM((D_EXPERT, D_MODEL), F32),
                            pltpu.VMEM((D_MODEL, 2 * D_EXPERT), BF16),
                            pltpu.VMEM((D_EXPERT, D_MODEL), BF16),
                            pltpu.SMEM((2 * n,), I32),
                            pltpu.SMEM((2 * n,), I32),
                            pltpu.SMEM((N_EXPERTS + 1,), I32),
                            pltpu.SMEM((N_EXPERTS,), I32),
                            pltpu.SMEM((n_tiles,), I32),
                            pltpu.SMEM((1,), I32),
                            pltpu.SemaphoreType.DMA((2,)), pltpu.SemaphoreType.DMA((2,)),
                            pltpu.SemaphoreType.DMA((3,))],
        ),
        out_shape=jax.ShapeDtypeStruct((2 * n * ROW_PITCH, LANES), F32),
        compiler_params=_params(("arbitrary",)),
        name="moe",
    )(dest, counts, v, wg, wu, wd)


def _ple_kernel(last, x1_ref, ya_ref, yb_ref, rwt_ref, p_ref, ng_ref, wg_ref, wp_ref, gn_ref, *out_refs):
    tm = TM_PLE
    rwt = rwt_ref[...]
    x2 = (x1_ref[...] + rwt[:, 0:1] * _from_chunked(ya_ref, tm) + rwt[:, 1:2] * _from_chunked(yb_ref, tm))
    h = _rms(x2, ng_ref[...]).astype(BF16)
    gate = _sigmoid(_dot(h, wg_ref[...]))
    ple = _dot(p_ref[...].astype(BF16), wp_ref[...])
    x3 = x2 + gate * ple
    if last:
        out_refs[0][...] = _rms(x3, gn_ref[...])
    else:
        out_refs[0][...] = x3
        out_refs[1][...] = _rms(x3, gn_ref[...]).astype(BF16)


def _ple_call(x1, y2, rwt, p_l, ng, wg, wp, gn, last):
    n = x1.shape[0]
    tm = TM_PLE
    const = lambda *shape: pl.BlockSpec(shape, lambda i: (0,) * len(shape))
    row = lambda width: pl.BlockSpec((tm, width), lambda i: (i, 0))
    if last:
        out_specs = [row(D_MODEL)]
        out_shape = [jax.ShapeDtypeStruct((n, D_MODEL), F32)]
    else:
        out_specs = [row(D_MODEL), row(D_MODEL)]
        out_shape = [jax.ShapeDtypeStruct((n, D_MODEL), F32), jax.ShapeDtypeStruct((n, D_MODEL), BF16)]
    return pl.pallas_call(
        functools.partial(_ple_kernel, last),
        grid=(n // tm,),
        in_specs=[row(D_MODEL),
                  pl.BlockSpec((tm * ROW_PITCH, LANES), lambda i: (i, 0)),
                  pl.BlockSpec((tm * ROW_PITCH, LANES), lambda i: (i + n // tm, 0)),
                  row(ROUTER_LANES), row(p_l.shape[1]), const(1, D_MODEL), const(D_MODEL, D_MODEL),
                  const(p_l.shape[1], D_MODEL), const(1, D_MODEL)],
        out_specs=out_specs,
        out_shape=out_shape,
        compiler_params=_params(("arbitrary",)),
        name="ple",
    )(x1, y2, y2, rwt, p_l, ng, wg, wp, gn)


def kernel(x, p, norm_mix_g, w_in, b_glu, conv_w, conv_b, conv_ln_g, conv_ln_b, w_conv_out, pool_w, pool_scale, w_out, norm_ffn_g, router_group_w, router_group_b, router_expert_w, router_expert_b, expert_w_gate, expert_w_up, expert_w_down, norm_ple_g, ple_gate_w, ple_proj_w, final_norm_g):
    batch, seq, d = x.shape
    depth = w_in.shape[0]
    n = batch * seq
    assert d == D_MODEL and seq % TM_MIX == 0 and n % TM_IN == 0 and (2 * n) % TM_MOE == 0

    xf = x.reshape(n, d)
    pf = p.reshape(depth, n, p.shape[-1])
    row2 = lambda a: a.reshape(1, -1)
    b_glu3 = b_glu.reshape(depth, 1, -1)
    wg_all = expert_w_gate.reshape(depth * N_EXPERTS, D_MODEL, D_EXPERT)
    wu_all = expert_w_up.reshape(depth * N_EXPERTS, D_MODEL, D_EXPERT)
    wd_all = expert_w_down.reshape(depth * N_EXPERTS, D_EXPERT, D_MODEL)

    u = _norm_call(xf, row2(norm_mix_g[0]))
    out = None
    for l in range(depth):
        glu = _glu_call(u, w_in, b_glu3, l)
        z = _z_call(u, w_in, l)

        pad = ROUTER_LANES - N_EXPERTS - N_GROUPS
        rw = jnp.concatenate([router_expert_w[l], router_group_w[l], jnp.zeros((d, pad), F32)], axis=1)
        rb = jnp.concatenate([router_expert_b[l], router_group_b[l], jnp.zeros((pad,), F32)])
        rw_hi = rw.astype(BF16)
        rw_lo = (rw - rw_hi.astype(F32)).astype(BF16)
        x1, v, eid, rwt, cnt = _mix_call(
            xf, glu, z, conv_w[l], row2(conv_b[l]), row2(conv_ln_g[l]), row2(conv_ln_b[l]),
            w_conv_out[l].astype(BF16), pool_w[l].astype(BF16), row2(pool_scale[l]),
            w_out[l].astype(BF16), row2(norm_ffn_g[l]), rw_hi, rw_lo, row2(rb), seq)

        cnt_row = cnt[0:1, :]
        offs = jnp.cumsum(cnt_row, axis=1) - cnt_row
        dest = _plan_call(eid, offs)[:, :2].reshape(-1)
        y2 = _moe_call(v, wg_all, wu_all, wd_all, l, dest, cnt_row[0, :N_EXPERTS].astype(I32))

        last = l == depth - 1
        gn = final_norm_g if last else norm_mix_g[l + 1]
        res = _ple_call(x1, y2, rwt, pf[l], row2(norm_ple_g[l]), ple_gate_w[l].astype(BF16),
                        ple_proj_w[l].astype(BF16), row2(gn), last)
        if last:
            out = res[0]
        else:
            xf, u = res
    return out.reshape(batch, seq, d)
```

```python
import functools

import jax
import jax.numpy as jnp
from jax import lax
from jax.experimental import pallas as pl
from jax.experimental.pallas import tpu as pltpu

F32 = jnp.float32
BF16 = jnp.bfloat16
I32 = jnp.int32

D_MODEL = 2048
CONV_WIDTH = 1024
CONV_KERNEL = 31
POOL_WIDTH = 1024
POOL_WINDOWS = (2, 4, 8, 16)
POOL_GROUP_IN = 256
POOL_GROUP_OUT = 512
N_GROUPS = 4
EXPERTS_PER_GROUP = 8
N_EXPERTS = N_GROUPS * EXPERTS_PER_GROUP
D_EXPERT = 256
EPS = 1e-6

SUBLANES = 8
LANES = 128
CHUNKS = D_MODEL // LANES
ROW_PITCH = CHUNKS + 4
VMEM_LIMIT = 56 * 1024 * 1024

TM_IN = 1024
TN_IN = 1024
TN_GLU = 512
TM_MIX = 256
TM_PLAN = 512
TM_PLE = 256
TM_MOE = 256
CONV_HALO = 32
POOL_HALO = 16
CONV_ROWS = 64
ROUTER_LANES = 128


def _rms(x, g):
    return x * lax.rsqrt(jnp.mean(x * x, axis=-1, keepdims=True) + EPS) * g


def _sigmoid(x):
    return 1.0 / (1.0 + jnp.exp(-x))


def _dot(a, b):
    return jnp.dot(a, b, preferred_element_type=F32)


def _params(sem):
    return pltpu.CompilerParams(dimension_semantics=sem, vmem_limit_bytes=VMEM_LIMIT)


def _to_chunked(ref, x):
    rows = x.shape[0]
    for c in range(CHUNKS):
        ref[pl.ds(c, rows, stride=ROW_PITCH), :] = x[:, c * LANES:(c + 1) * LANES]


def _zero_pads(ref, rows):
    for c in range(CHUNKS, ROW_PITCH):
        ref[pl.ds(c, rows, stride=ROW_PITCH), :] = jnp.zeros((rows, LANES), F32)


def _from_chunked(ref, rows):
    return jnp.concatenate([ref[pl.ds(c, rows, stride=ROW_PITCH), :] for c in range(CHUNKS)], axis=-1)


def _norm_kernel(x_ref, g_ref, o_ref):
    o_ref[...] = _rms(x_ref[...], g_ref[...]).astype(o_ref.dtype)


def _norm_call(x, g):
    n = x.shape[0]
    tm = 512
    return pl.pallas_call(
        _norm_kernel,
        grid=(n // tm,),
        in_specs=[pl.BlockSpec((tm, D_MODEL), lambda i: (i, 0)),
                  pl.BlockSpec((1, D_MODEL), lambda i: (0, 0))],
        out_specs=pl.BlockSpec((tm, D_MODEL), lambda i: (i, 0)),
        out_shape=jax.ShapeDtypeStruct((n, D_MODEL), BF16),
        compiler_params=_params(("arbitrary",)),
        name="norm",
    )(x, g)


def _glu_kernel(u_ref, w1_ref, w2_ref, b1_ref, b2_ref, o_ref, w1s, w2s):
    @pl.when(pl.program_id(1) == 0)
    def _():
        w1s[...] = w1_ref[...].astype(BF16)
        w2s[...] = w2_ref[...].astype(BF16)

    u = u_ref[...]
    a = _dot(u, w1s[...]) + b1_ref[...]
    g = _dot(u, w2s[...]) + b2_ref[...]
    o_ref[...] = (a * _sigmoid(g)).astype(o_ref.dtype)


def _glu_call(u, w_in, b_glu, layer):
    n = u.shape[0]
    nj = CONV_WIDTH // TN_GLU
    return pl.pallas_call(
        _glu_kernel,
        grid=(nj, n // TM_IN),
        in_specs=[pl.BlockSpec((TM_IN, D_MODEL), lambda j, i: (i, 0)),
                  pl.BlockSpec((None, D_MODEL, TN_GLU), lambda j, i: (layer, 0, j)),
                  pl.BlockSpec((None, D_MODEL, TN_GLU), lambda j, i: (layer, 0, j + nj)),
                  pl.BlockSpec((None, 1, TN_GLU), lambda j, i: (layer, 0, j)),
                  pl.BlockSpec((None, 1, TN_GLU), lambda j, i: (layer, 0, j + nj))],
        out_specs=pl.BlockSpec((TM_IN, TN_GLU), lambda j, i: (i, j)),
        out_shape=jax.ShapeDtypeStruct((n, CONV_WIDTH), BF16),
        scratch_shapes=[pltpu.VMEM((D_MODEL, TN_GLU), BF16), pltpu.VMEM((D_MODEL, TN_GLU), BF16)],
        compiler_params=_params(("arbitrary", "arbitrary")),
        name="glu",
    )(u, w_in, w_in, b_glu, b_glu)


def _z_kernel(u_ref, w_ref, o_ref, ws):
    j = pl.program_id(0)

    @pl.when(pl.program_id(1) == 0)
    def _():
        ws[...] = w_ref[...].astype(BF16)

    z = _dot(u_ref[...], ws[...])

    @pl.when(j == 0)
    def _():
        o_ref[...] = z.astype(o_ref.dtype)

    @pl.when(j > 0)
    def _():
        o_ref[...] = _sigmoid(z).astype(o_ref.dtype)


def _z_call(u, w_in, layer):
    n = u.shape[0]
    width = POOL_WIDTH + 2 * D_MODEL
    col0 = (2 * CONV_WIDTH) // TN_IN
    return pl.pallas_call(
        _z_kernel,
        grid=(width // TN_IN, n // TM_IN),
        in_specs=[pl.BlockSpec((TM_IN, D_MODEL), lambda j, i: (i, 0)),
                  pl.BlockSpec((None, D_MODEL, TN_IN), lambda j, i: (layer, 0, j + col0))],
        out_specs=pl.BlockSpec((TM_IN, TN_IN), lambda j, i: (i, j)),
        out_shape=jax.ShapeDtypeStruct((n, width), BF16),
        scratch_shapes=[pltpu.VMEM((D_MODEL, TN_IN), BF16)],
        compiler_params=_params(("arbitrary", "arbitrary")),
        name="zproj",
    )(u, w_in)


def _conv_taps(r):
    lead = CONV_HALO - (CONV_KERNEL - 1)
    return [(q, SUBLANES * q + r - lead) for q in range((lead + CONV_KERNEL) // SUBLANES + 1)
            if 0 <= SUBLANES * q + r - lead < CONV_KERNEL]


def _mix_kernel(tiles_per_seq,
                x_ref, glu_ref, z_ref, cw_ref, cb_ref, lng_ref, lnb_ref, wco_ref, pw_ref,
                ps_ref, wo_ref, ng_ref, rwh_ref, rwl_ref, rb_ref,
                x1_ref, v_ref, eid_ref, rwt_ref, cnt_ref,
                cbuf, pbuf, conv_s):
    tm = TM_MIX
    seq_tile = pl.program_id(0) % tiles_per_seq

    @pl.when(seq_tile == 0)
    def _():
        cbuf[:, 0:CONV_HALO, :] = jnp.zeros((CONV_WIDTH // LANES, CONV_HALO, LANES), F32)
        pbuf[:, 0:POOL_HALO, :] = jnp.zeros((POOL_WIDTH // LANES, POOL_HALO, LANES), F32)

    for c in range(CONV_WIDTH // LANES):
        cbuf[c, CONV_HALO:CONV_HALO + tm, :] = glu_ref[:, c * LANES:(c + 1) * LANES].astype(F32)
    for c in range(POOL_WIDTH // LANES):
        pbuf[c, POOL_HALO:POOL_HALO + tm, :] = z_ref[:, c * LANES:(c + 1) * LANES].astype(F32)

    for c in range(CONV_WIDTH // LANES):
        ls = slice(c * LANES, (c + 1) * LANES)
        for rb in range(tm // CONV_ROWS):
            acc = None
            for r in range(SUBLANES):
                taps = _conv_taps(r)
                qmin, qmax = taps[0][0], taps[-1][0]
                rows = CONV_ROWS + SUBLANES * (qmax - qmin)
                a = cbuf[c, pl.ds(rb * CONV_ROWS + SUBLANES * qmin + r, rows), :]
                for q, k in taps:
                    o = SUBLANES * (q - qmin)
                    term = a[o:o + CONV_ROWS, :] * cw_ref[k:k + 1, ls]
                    acc = term if acc is None else acc + term
            conv_s[rb * CONV_ROWS:(rb + 1) * CONV_ROWS, ls] = acc + cb_ref[:, ls]
    cbuf[:, 0:CONV_HALO, :] = cbuf[:, tm:tm + CONV_HALO, :]

    a = conv_s[...]
    mu = jnp.mean(a, axis=-1, keepdims=True)
    ac = a - mu
    var = jnp.mean(ac * ac, axis=-1, keepdims=True)
    a = ac * lax.rsqrt(var + EPS) * lng_ref[...] + lnb_ref[...]
    a = a * _sigmoid(a)
    branch_a = _dot(a.astype(BF16), wco_ref[...])

    pos1 = seq_tile * tm + lax.broadcasted_iota(I32, (tm, 1), 0) + 1
    merged = []
    for g, w in enumerate(POOL_WINDOWS):
        cnt = jnp.minimum(pos1, w).astype(F32)
        halves = []
        for s in range(g * POOL_GROUP_IN // LANES, (g + 1) * POOL_GROUP_IN // LANES):
            cur = pbuf[s, pl.ds(POOL_HALO, tm), :]
            acc = cur
            for d in range(1, w):
                acc = acc + pbuf[s, pl.ds(POOL_HALO - d, tm), :]
            halves.append(acc / cnt - cur)
        pooled = jnp.concatenate(halves, axis=-1)
        os_ = slice(g * POOL_GROUP_OUT, (g + 1) * POOL_GROUP_OUT)
        yb = _dot(pooled.astype(BF16), pw_ref[g]) * ps_ref[:, os_]
        ga = z_ref[:, POOL_WIDTH + g * POOL_GROUP_OUT:POOL_WIDTH + (g + 1) * POOL_GROUP_OUT]
        gb = z_ref[:, POOL_WIDTH + D_MODEL + g * POOL_GROUP_OUT:
                   POOL_WIDTH + D_MODEL + (g + 1) * POOL_GROUP_OUT]
        m = ga.astype(F32) * branch_a[:, os_] + gb.astype(F32) * yb
        merged.append(m.astype(BF16))
    pbuf[:, 0:POOL_HALO, :] = pbuf[:, tm:tm + POOL_HALO, :]
    merged = jnp.concatenate(merged, axis=-1)

    x1 = x_ref[...] + _dot(merged, wo_ref[...])
    x1_ref[...] = x1

    v = _rms(x1, ng_ref[...])
    _to_chunked(v_ref, v)
    _zero_pads(v_ref, tm)
    v_hi = v.astype(BF16)
    v_lo = (v - v_hi.astype(F32)).astype(BF16)
    logits = (_dot(v_hi, rwh_ref[...]) + _dot(v_lo, rwh_ref[...]) + _dot(v_hi, rwl_ref[...])
              + rb_ref[...])
    lane = lax.broadcasted_iota(I32, (tm, ROUTER_LANES), 1)
    lane_f = lane.astype(F32)
    neg = jnp.float32(-jnp.inf)
    big = jnp.float32(1e9)

    is_group = jnp.logical_and(lane >= N_EXPERTS, lane < N_EXPERTS + N_GROUPS)
    gl = jnp.where(is_group, logits, neg)
    gmax = jnp.max(gl, axis=-1, keepdims=True)
    gsum = jnp.sum(jnp.exp(gl - gmax), axis=-1, keepdims=True)
    gidx = jnp.min(jnp.where(gl == gmax, lane_f, big), axis=-1, keepdims=True) - N_EXPERTS
    g_val = 1.0 / gsum

    lane_group = jnp.right_shift(lane, 3).astype(F32)
    in_group = jnp.logical_and(lane < N_EXPERTS, lane_group == gidx)
    el = jnp.where(in_group, logits, neg)
    m1 = jnp.max(el, axis=-1, keepdims=True)
    i1 = jnp.min(jnp.where(el == m1, lane_f, big), axis=-1, keepdims=True)
    el2 = jnp.where(lane_f == i1, neg, el)
    m2 = jnp.max(el2, axis=-1, keepdims=True)
    i2 = jnp.min(jnp.where(el2 == m2, lane_f, big), axis=-1, keepdims=True)
    t = jnp.exp(m2 - m1)
    e1 = 1.0 / (1.0 + t)
    e2 = t / (1.0 + t)
    eid_ref[...] = jnp.where(lane == 0, i1, jnp.where(lane == 1, i2, 0.0)).astype(I32)
    rwt_ref[...] = jnp.where(lane == 0, g_val * e1, jnp.where(lane == 1, g_val * e2, 0.0))

    @pl.when(pl.program_id(0) == 0)
    def _():
        cnt_ref[...] = jnp.zeros(cnt_ref.shape, F32)

    picked = jnp.where(jnp.logical_or(lane_f == i1, lane_f == i2), 1.0, 0.0)
    cnt_ref[0:1, :] += jnp.sum(picked, axis=0, keepdims=True)


def _mix_call(x, glu, z, cw, cb, lng, lnb, wco, pw, ps, wo, ng, rwh, rwl, rb, seq_len):
    n = x.shape[0]
    tm = TM_MIX
    zw = z.shape[1]
    const = lambda *shape: pl.BlockSpec(shape, lambda i: (0,) * len(shape))
    row = lambda width: pl.BlockSpec((tm, width), lambda i: (i, 0))
    return pl.pallas_call(
        functools.partial(_mix_kernel, seq_len // tm),
        grid=(n // tm,),
        in_specs=[row(D_MODEL), row(CONV_WIDTH), row(zw),
                  const(CONV_KERNEL, CONV_WIDTH), const(1, CONV_WIDTH), const(1, CONV_WIDTH),
                  const(1, CONV_WIDTH), const(CONV_WIDTH, D_MODEL),
                  const(len(POOL_WINDOWS), POOL_GROUP_IN, POOL_GROUP_OUT), const(1, D_MODEL),
                  const(D_MODEL, D_MODEL), const(1, D_MODEL),
                  const(D_MODEL, ROUTER_LANES), const(D_MODEL, ROUTER_LANES), const(1, ROUTER_LANES)],
        out_specs=[row(D_MODEL), pl.BlockSpec((tm * ROW_PITCH, LANES), lambda i: (i, 0)),
                   row(ROUTER_LANES), row(ROUTER_LANES), const(SUBLANES, ROUTER_LANES)],
        out_shape=[jax.ShapeDtypeStruct((n, D_MODEL), F32),
                   jax.ShapeDtypeStruct((n * ROW_PITCH, LANES), F32),
                   jax.ShapeDtypeStruct((n, ROUTER_LANES), I32),
                   jax.ShapeDtypeStruct((n, ROUTER_LANES), F32),
                   jax.ShapeDtypeStruct((SUBLANES, ROUTER_LANES), F32)],
        scratch_shapes=[pltpu.VMEM((CONV_WIDTH // LANES, CONV_HALO + tm, LANES), F32),
                        pltpu.VMEM((POOL_WIDTH // LANES, POOL_HALO + tm, LANES), F32),
                        pltpu.VMEM((tm, CONV_WIDTH), F32)],
        compiler_params=_params(("arbitrary",)),
        name="mix",
    )(x, glu, z, cw, cb, lng, lnb, wco, pw, ps, wo, ng, rwh, rwl, rb)


def _plan_kernel(eid_ref, offs_ref, dest_ref, carry):
    tp = TM_PLAN

    @pl.when(pl.program_id(0) == 0)
    def _():
        carry[...] = jnp.zeros(carry.shape, F32)

    eid = eid_ref[...]
    lane = lax.broadcasted_iota(I32, (tp, ROUTER_LANES), 1)
    hit0 = lane == eid[:, 0:1]
    hit1 = lane == eid[:, 1:2]
    onehot = jnp.where(jnp.logical_or(hit0, hit1), 1.0, 0.0)
    r = lax.broadcasted_iota(I32, (tp, tp), 0)
    c = lax.broadcasted_iota(I32, (tp, tp), 1)
    tri = jnp.where(c < r, 1.0, 0.0).astype(BF16)
    before = _dot(tri, onehot.astype(BF16)) + carry[0:1, :] + offs_ref[...]
    d0 = jnp.sum(jnp.where(hit0, before, 0.0), axis=-1, keepdims=True)
    d1 = jnp.sum(jnp.where(hit1, before, 0.0), axis=-1, keepdims=True)
    dest_ref[...] = jnp.where(lane == 0, d0, jnp.where(lane == 1, d1, 0.0)).astype(I32)
    carry[0:1, :] += jnp.sum(onehot, axis=0, keepdims=True)


def _plan_call(eid, offs):
    n = eid.shape[0]
    tp = TM_PLAN
    return pl.pallas_call(
        _plan_kernel,
        grid=(n // tp,),
        in_specs=[pl.BlockSpec((tp, ROUTER_LANES), lambda i: (i, 0)),
                  pl.BlockSpec((1, ROUTER_LANES), lambda i: (0, 0))],
        out_specs=pl.BlockSpec((tp, ROUTER_LANES), lambda i: (i, 0)),
        out_shape=jax.ShapeDtypeStruct((n, ROUTER_LANES), I32),
        scratch_shapes=[pltpu.VMEM((SUBLANES, ROUTER_LANES), F32)],
        compiler_params=_params(("arbitrary",)),
        name="plan",
    )(eid, offs)


def _moe_kernel(layer, n_tok,
                dest_ref, cnt_ref,
                v_hbm, wg_hbm, wu_hbm, wd_hbm, out_hbm,
                xbuf, ybuf, xb_s, stg_g, stg_u, stg_d, wgu_s, wd_s,
                row_src, row_dst, offs_s, nxt_s, first_s, cur_s,
                gsem, ssem, wsem):
    tm = TM_MOE
    j = pl.program_id(0)
    n_tiles = pl.num_programs(0)
    slot = j % 2
    row0 = j * tm

    def gather_row(tile, s, r):
        return pltpu.make_async_copy(v_hbm.at[pl.ds(row_src[tile * tm + r], CHUNKS)],
                                     xbuf.at[s, pl.ds(r * ROW_PITCH, CHUNKS)], gsem.at[s])

    def scatter_row(tile, s, r):
        return pltpu.make_async_copy(ybuf.at[s, pl.ds(r * ROW_PITCH, ROW_PITCH)],
                                     out_hbm.at[pl.ds(row_dst[tile * tm + r], ROW_PITCH)], ssem.at[s])

    def start_rows(make, tile, s):
        def body(r, c):
            make(tile, s, r).start()
            return c
        lax.fori_loop(0, tm, body, 0, unroll=8)

    def wait_gather(s):
        pltpu.make_async_copy(v_hbm.at[pl.ds(0, tm * CHUNKS)], xbuf.at[s, pl.ds(0, tm * CHUNKS)],
                              gsem.at[s]).wait()

    def wait_scatter(s):
        pltpu.make_async_copy(ybuf.at[s], out_hbm.at[pl.ds(0, tm * ROW_PITCH)], ssem.at[s]).wait()

    def weight_copies(e):
        w = layer * N_EXPERTS + e
        return [pltpu.make_async_copy(wg_hbm.at[w], stg_g, wsem.at[0]),
                pltpu.make_async_copy(wu_hbm.at[w], stg_u, wsem.at[1]),
                pltpu.make_async_copy(wd_hbm.at[w], stg_d, wsem.at[2])]

    @pl.when(j == 0)
    def _():
        def off_body(e, acc):
            offs_s[e] = acc
            return acc + cnt_ref[e]
        offs_s[N_EXPERTS] = lax.fori_loop(0, N_EXPERTS, off_body, 0)

        def nxt_body(k, nxt):
            e = N_EXPERTS - 1 - k
            nxt_s[e] = nxt
            return jnp.where(cnt_ref[e] > 0, e, nxt)
        first_expert = lax.fori_loop(0, N_EXPERTS, nxt_body, N_EXPERTS)

        def first_body(t, e):
            e = lax.while_loop(lambda e: offs_s[e + 1] <= t * tm, lambda e: e + 1, e)
            first_s[t] = e
            return e
        lax.fori_loop(0, n_tiles, first_body, 0)

        def inv_body(i, c):
            base = i * (4 * ROW_PITCH)
            for k in range(8):
                r = dest_ref[i * 8 + k]
                row_src[r] = base + (k // 2) * ROW_PITCH
                row_dst[r] = base + ((k % 2) * n_tok + k // 2) * ROW_PITCH
            return c
        lax.fori_loop(0, (2 * n_tok) // 8, inv_body, 0)

        for s in range(2):
            _zero_pads(ybuf.at[s], tm)
        cur_s[0] = -1
        for cp in weight_copies(first_expert):
            cp.start()
        start_rows(gather_row, 0, 0)

    wait_gather(slot)

    @pl.when(j + 1 < n_tiles)
    def _():
        start_rows(gather_row, j + 1, 1 - slot)

    xb_s[...] = _from_chunked(xbuf.at[slot], tm).astype(BF16)
    rows = lax.broadcasted_iota(I32, (tm, 1), 0)

    @pl.when(j >= 2)
    def _():
        wait_scatter(slot)

    def expert_step(e):
        @pl.when(cnt_ref[e] > 0)
        def _():
            @pl.when(cur_s[0] != e)
            def _():
                for cp in weight_copies(e):
                    cp.wait()
                wgu_s[:, 0:D_EXPERT] = stg_g[...].astype(BF16)
                wgu_s[:, D_EXPERT:2 * D_EXPERT] = stg_u[...].astype(BF16)
                wd_s[...] = stg_d[...].astype(BF16)
                cur_s[0] = e
                nxt = nxt_s[e]

                @pl.when(nxt < N_EXPERTS)
                def _():
                    for cp in weight_copies(nxt):
                        cp.start()

            lo = offs_s[e] - row0
            hi = offs_s[e + 1] - row0
            gu = _dot(xb_s[...], wgu_s[...])
            gate = gu[:, 0:D_EXPERT]
            h = gate * _sigmoid(gate) * gu[:, D_EXPERT:2 * D_EXPERT]
            h = jnp.where(jnp.logical_and(rows >= lo, rows < hi), h, 0.0)
            y = _dot(h.astype(BF16), wd_s[...])

            @pl.when(e == first_s[j])
            def _():
                _to_chunked(ybuf.at[slot], y)

            @pl.when(e != first_s[j])
            def _():
                _to_chunked(ybuf.at[slot], _from_chunked(ybuf.at[slot], tm) + y)
        return e + 1

    lax.while_loop(lambda e: offs_s[e] < row0 + tm, expert_step, first_s[j])

    start_rows(scatter_row, j, slot)

    @pl.when(j == n_tiles - 1)
    def _():
        wait_scatter(slot)
        wait_scatter(1 - slot)


def _moe_call(v, wg, wu, wd, layer, dest, counts):
    n = v.shape[0] // ROW_PITCH
    tm = TM_MOE
    n_tiles = (2 * n) // tm
    any_spec = pl.BlockSpec(memory_space=pl.ANY)
    return pl.pallas_call(
        functools.partial(_moe_kernel, layer, n),
        grid_spec=pltpu.PrefetchScalarGridSpec(
            num_scalar_prefetch=2,
            grid=(n_tiles,),
            in_specs=[any_spec, any_spec, any_spec, any_spec],
            out_specs=any_spec,
            scratch_shapes=[pltpu.VMEM((2, tm * ROW_PITCH, LANES), F32),
                            pltpu.VMEM((2, tm * ROW_PITCH, LANES), F32),
                            pltpu.VMEM((tm, D_MODEL), BF16),
                            pltpu.VMEM((D_MODEL, D_EXPERT), F32),
                            pltpu.VMEM((D_MODEL, D_EXPERT), F32),
                            pltpu.VMEM((D_EXPERT, D_MODEL), F32),
                            pltpu.VMEM((D_MODEL, 2 * D_EXPERT), BF16),
                            pltpu.VMEM((D_EXPERT, D_MODEL), BF16),
                            pltpu.SMEM((2 * n,), I32),
                            pltpu.SMEM((2 * n,), I32),
                            pltpu.SMEM((N_EXPERTS + 1,), I32),
                            pltpu.SMEM((N_EXPERTS,), I32),
                            pltpu.SMEM((n_tiles,), I32),
                            pltpu.SMEM((1,), I32),
                            pltpu.SemaphoreType.DMA((2,)), pltpu.SemaphoreType.DMA((2,)),
                            pltpu.SemaphoreType.DMA((3,))],
        ),
        out_shape=jax.ShapeDtypeStruct((2 * n * ROW_PITCH, LANES), F32),
        compiler_params=_params(("arbitrary",)),
        name="moe",
    )(dest, counts, v, wg, wu, wd)


def _ple_kernel(last, x1_ref, ya_ref, yb_ref, rwt_ref, p_ref, ng_ref, wg_ref, wp_ref, gn_ref, *out_refs):
    tm = TM_PLE
    rwt = rwt_ref[...]
    x2 = (x1_ref[...] + rwt[:, 0:1] * _from_chunked(ya_ref, tm) + rwt[:, 1:2] * _from_chunked(yb_ref, tm))
    h = _rms(x2, ng_ref[...]).astype(BF16)
    gate = _sigmoid(_dot(h, wg_ref[...]))
    ple = _dot(p_ref[...].astype(BF16), wp_ref[...])
    x3 = x2 + gate * ple
    if last:
        out_refs[0][...] = _rms(x3, gn_ref[...])
    else:
        out_refs[0][...] = x3
        out_refs[1][...] = _rms(x3, gn_ref[...]).astype(BF16)


def _ple_call(x1, y2, rwt, p_l, ng, wg, wp, gn, last):
    n = x1.shape[0]
    tm = TM_PLE
    const = lambda *shape: pl.BlockSpec(shape, lambda i: (0,) * len(shape))
    row = lambda width: pl.BlockSpec((tm, width), lambda i: (i, 0))
    if last:
        out_specs = [row(D_MODEL)]
        out_shape = [jax.ShapeDtypeStruct((n, D_MODEL), F32)]
    else:
        out_specs = [row(D_MODEL), row(D_MODEL)]
        out_shape = [jax.ShapeDtypeStruct((n, D_MODEL), F32), jax.ShapeDtypeStruct((n, D_MODEL), BF16)]
    return pl.pallas_call(
        functools.partial(_ple_kernel, last),
        grid=(n // tm,),
        in_specs=[row(D_MODEL),
                  pl.BlockSpec((tm * ROW_PITCH, LANES), lambda i: (i, 0)),
                  pl.BlockSpec((tm * ROW_PITCH, LANES), lambda i: (i + n // tm, 0)),
                  row(ROUTER_LANES), row(p_l.shape[1]), const(1, D_MODEL), const(D_MODEL, D_MODEL),
                  const(p_l.shape[1], D_MODEL), const(1, D_MODEL)],
        out_specs=out_specs,
        out_shape=out_shape,
        compiler_params=_params(("arbitrary",)),
        name="ple",
    )(x1, y2, y2, rwt, p_l, ng, wg, wp, gn)


def kernel(x, p, norm_mix_g, w_in, b_glu, conv_w, conv_b, conv_ln_g, conv_ln_b, w_conv_out, pool_w, pool_scale, w_out, norm_ffn_g, router_group_w, router_group_b, router_expert_w, router_expert_b, expert_w_gate, expert_w_up, expert_w_down, norm_ple_g, ple_gate_w, ple_proj_w, final_norm_g):
    batch, seq, d = x.shape
    depth = w_in.shape[0]
    n = batch * seq
    assert d == D_MODEL and seq % TM_MIX == 0 and n % TM_IN == 0 and (2 * n) % TM_MOE == 0

    xf = x.reshape(n, d)
    pf = p.reshape(depth, n, p.shape[-1])
    row2 = lambda a: a.reshape(1, -1)
    b_glu3 = b_glu.reshape(depth, 1, -1)
    wg_all = expert_w_gate.reshape(depth * N_EXPERTS, D_MODEL, D_EXPERT)
    wu_all = expert_w_up.reshape(depth * N_EXPERTS, D_MODEL, D_EXPERT)
    wd_all = expert_w_down.reshape(depth * N_EXPERTS, D_EXPERT, D_MODEL)

    u = _norm_call(xf, row2(norm_mix_g[0]))
    out = None
    for l in range(depth):
        glu = _glu_call(u, w_in, b_glu3, l)
        z = _z_call(u, w_in, l)

        pad = ROUTER_LANES - N_EXPERTS - N_GROUPS
        rw = jnp.concatenate([router_expert_w[l], router_group_w[l], jnp.zeros((d, pad), F32)], axis=1)
        rb = jnp.concatenate([router_expert_b[l], router_group_b[l], jnp.zeros((pad,), F32)])
        rw_hi = rw.astype(BF16)
        rw_lo = (rw - rw_hi.astype(F32)).astype(BF16)
        x1, v, eid, rwt, cnt = _mix_call(
            xf, glu, z, conv_w[l], row2(conv_b[l]), row2(conv_ln_g[l]), row2(conv_ln_b[l]),
            w_conv_out[l].astype(BF16), pool_w[l].astype(BF16), row2(pool_scale[l]),
            w_out[l].astype(BF16), row2(norm_ffn_g[l]), rw_hi, rw_lo, row2(rb), seq)

        cnt_row = cnt[0:1, :]
        offs = jnp.cumsum(cnt_row, axis=1) - cnt_row
        dest = _plan_call(eid, offs)[:, :2].reshape(-1)
        y2 = _moe_call(v, wg_all, wu_all, wd_all, l, dest, cnt_row[0, :N_EXPERTS].astype(I32))

        last = l == depth - 1
        gn = final_norm_g if last else norm_mix_g[l + 1]
        res = _ple_call(x1, y2, rwt, pf[l], row2(norm_ple_g[l]), ple_gate_w[l].astype(BF16),
                        ple_proj_w[l].astype(BF16), row2(gn), last)
        if last:
            out = res[0]
        else:
            xf, u = res
    return out.reshape(batch, seq, d)
```

```python
import functools

import jax
import jax.numpy as jnp
from jax import lax
from jax.experimental import pallas as pl
from jax.experimental.pallas import tpu as pltpu

F32 = jnp.float32
BF16 = jnp.bfloat16
I32 = jnp.int32

D_MODEL = 2048
CONV_WIDTH = 1024
CONV_KERNEL = 31
POOL_WIDTH = 1024
POOL_WINDOWS = (2, 4, 8, 16)
POOL_GROUP_IN = 256
POOL_GROUP_OUT = 512
N_GROUPS = 4
EXPERTS_PER_GROUP = 8
N_EXPERTS = N_GROUPS * EXPERTS_PER_GROUP
D_EXPERT = 256
EPS = 1e-6

SUBLANES = 8
LANES = 128
CHUNKS = D_MODEL // LANES
ROW_PITCH = CHUNKS + 4
VMEM_LIMIT = 56 * 1024 * 1024

TM_IN = 1024
TN_IN = 1024
TN_GLU = 512
TM_MIX = 256
TM_PLAN = 512
TM_PLE = 256
TM_MOE = 256
CONV_HALO = 32
POOL_HALO = 16
CONV_ROWS = 64
ROUTER_LANES = 128


def _rms(x, g):
    return x * lax.rsqrt(jnp.mean(x * x, axis=-1, keepdims=True) + EPS) * g


def _sigmoid(x):
    return 1.0 / (1.0 + jnp.exp(-x))


def _dot(a, b):
    return jnp.dot(a, b, preferred_element_type=F32)


def _params(sem):
    return pltpu.CompilerParams(dimension_semantics=sem, vmem_limit_bytes=VMEM_LIMIT)


def _to_chunked(ref, x):
    rows = x.shape[0]
    for c in range(CHUNKS):
        ref[pl.ds(c, rows, stride=ROW_PITCH), :] = x[:, c * LANES:(c + 1) * LANES]


def _zero_pads(ref, rows):
    for c in range(CHUNKS, ROW_PITCH):
        ref[pl.ds(c, rows, stride=ROW_PITCH), :] = jnp.zeros((rows, LANES), F32)


def _from_chunked(ref, rows):
    return jnp.concatenate([ref[pl.ds(c, rows, stride=ROW_PITCH), :] for c in range(CHUNKS)], axis=-1)


def _norm_kernel(x_ref, g_ref, o_ref):
    o_ref[...] = _rms(x_ref[...], g_ref[...]).astype(o_ref.dtype)


def _norm_call(x, g):
    n = x.shape[0]
    tm = 512
    return pl.pallas_call(
        _norm_kernel,
        grid=(n // tm,),
        in_specs=[pl.BlockSpec((tm, D_MODEL), lambda i: (i, 0)),
                  pl.BlockSpec((1, D_MODEL), lambda i: (0, 0))],
        out_specs=pl.BlockSpec((tm, D_MODEL), lambda i: (i, 0)),
        out_shape=jax.ShapeDtypeStruct((n, D_MODEL), BF16),
        compiler_params=_params(("arbitrary",)),
        name="norm",
    )(x, g)


def _glu_kernel(u_ref, w1_ref, w2_ref, b1_ref, b2_ref, o_ref, w1s, w2s):
    @pl.when(pl.program_id(1) == 0)
    def _():
        w1s[...] = w1_ref[...].astype(BF16)
        w2s[...] = w2_ref[...].astype(BF16)

    u = u_ref[...]
    a = _dot(u, w1s[...]) + b1_ref[...]
    g = _dot(u, w2s[...]) + b2_ref[...]
    o_ref[...] = (a * _sigmoid(g)).astype(o_ref.dtype)


def _glu_call(u, w_in, b_glu, layer):
    n = u.shape[0]
    nj = CONV_WIDTH // TN_GLU
    return pl.pallas_call(
        _glu_kernel,
        grid=(nj, n // TM_IN),
        in_specs=[pl.BlockSpec((TM_IN, D_MODEL), lambda j, i: (i, 0)),
                  pl.BlockSpec((None, D_MODEL, TN_GLU), lambda j, i: (layer, 0, j)),
                  pl.BlockSpec((None, D_MODEL, TN_GLU), lambda j, i: (layer, 0, j + nj)),
                  pl.BlockSpec((None, 1, TN_GLU), lambda j, i: (layer, 0, j)),
                  pl.BlockSpec((None, 1, TN_GLU), lambda j, i: (layer, 0, j + nj))],
        out_specs=pl.BlockSpec((TM_IN, TN_GLU), lambda j, i: (i, j)),
        out_shape=jax.ShapeDtypeStruct((n, CONV_WIDTH), BF16),
        scratch_shapes=[pltpu.VMEM((D_MODEL, TN_GLU), BF16), pltpu.VMEM((D_MODEL, TN_GLU), BF16)],
        compiler_params=_params(("arbitrary", "arbitrary")),
        name="glu",
    )(u, w_in, w_in, b_glu, b_glu)


def _z_kernel(u_ref, w_ref, o_ref, ws):
    j = pl.program_id(0)

    @pl.when(pl.program_id(1) == 0)
    def _():
        ws[...] = w_ref[...].astype(BF16)

    z = _dot(u_ref[...], ws[...])

    @pl.when(j == 0)
    def _():
        o_ref[...] = z.astype(o_ref.dtype)

    @pl.when(j > 0)
    def _():
        o_ref[...] = _sigmoid(z).astype(o_ref.dtype)


def _z_call(u, w_in, layer):
    n = u.shape[0]
    width = POOL_WIDTH + 2 * D_MODEL
    col0 = (2 * CONV_WIDTH) // TN_IN
    return pl.pallas_call(
        _z_kernel,
        grid=(width // TN_IN, n // TM_IN),
        in_specs=[pl.BlockSpec((TM_IN, D_MODEL), lambda j, i: (i, 0)),
                  pl.BlockSpec((None, D_MODEL, TN_IN), lambda j, i: (layer, 0, j + col0))],
        out_specs=pl.BlockSpec((TM_IN, TN_IN), lambda j, i: (i, j)),
        out_shape=jax.ShapeDtypeStruct((n, width), BF16),
        scratch_shapes=[pltpu.VMEM((D_MODEL, TN_IN), BF16)],
        compiler_params=_params(("arbitrary", "arbitrary")),
        name="zproj",
    )(u, w_in)


def _conv_taps(r):
    lead = CONV_HALO - (CONV_KERNEL - 1)
    return [(q, SUBLANES * q + r - lead) for q in range((lead + CONV_KERNEL) // SUBLANES + 1)
            if 0 <= SUBLANES * q + r - lead < CONV_KERNEL]


def _mix_kernel(tiles_per_seq,
                x_ref, glu_ref, z_ref, cw_ref, cb_ref, lng_ref, lnb_ref, wco_ref, pw_ref,
                ps_ref, wo_ref, ng_ref, rwh_ref, rwl_ref, rb_ref,
                x1_ref, v_ref, eid_ref, rwt_ref, cnt_ref,
                cbuf, pbuf, conv_s):
    tm = TM_MIX
    seq_tile = pl.program_id(0) % tiles_per_seq

    @pl.when(seq_tile == 0)
    def _():
        cbuf[:, 0:CONV_HALO, :] = jnp.zeros((CONV_WIDTH // LANES, CONV_HALO, LANES), F32)
        pbuf[:, 0:POOL_HALO, :] = jnp.zeros((POOL_WIDTH // LANES, POOL_HALO, LANES), F32)

    for c in range(CONV_WIDTH // LANES):
        cbuf[c, CONV_HALO:CONV_HALO + tm, :] = glu_ref[:, c * LANES:(c + 1) * LANES].astype(F32)
    for c in range(POOL_WIDTH // LANES):
        pbuf[c, POOL_HALO:POOL_HALO + tm, :] = z_ref[:, c * LANES:(c + 1) * LANES].astype(F32)

    for c in range(CONV_WIDTH // LANES):
        ls = slice(c * LANES, (c + 1) * LANES)
        for rb in range(tm // CONV_ROWS):
            acc = None
            for r in range(SUBLANES):
                taps = _conv_taps(r)
                qmin, qmax = taps[0][0], taps[-1][0]
                rows = CONV_ROWS + SUBLANES * (qmax - qmin)
                a = cbuf[c, pl.ds(rb * CONV_ROWS + SUBLANES * qmin + r, rows), :]
                for q, k in taps:
                    o = SUBLANES * (q - qmin)
                    term = a[o:o + CONV_ROWS, :] * cw_ref[k:k + 1, ls]
                    acc = term if acc is None else acc + term
            conv_s[rb * CONV_ROWS:(rb + 1) * CONV_ROWS, ls] = acc + cb_ref[:, ls]
    cbuf[:, 0:CONV_HALO, :] = cbuf[:, tm:tm + CONV_HALO, :]

    a = conv_s[...]
    mu = jnp.mean(a, axis=-1, keepdims=True)
    ac = a - mu
    var = jnp.mean(ac * ac, axis=-1, keepdims=True)
    a = ac * lax.rsqrt(var + EPS) * lng_ref[...] + lnb_ref[...]
    a = a * _sigmoid(a)
    branch_a = _dot(a.astype(BF16), wco_ref[...])

    pos1 = seq_tile * tm + lax.broadcasted_iota(I32, (tm, 1), 0) + 1
    merged = []
    for g, w in enumerate(POOL_WINDOWS):
        cnt = jnp.minimum(pos1, w).astype(F32)
        halves = []
        for s in range(g * POOL_GROUP_IN // LANES, (g + 1) * POOL_GROUP_IN // LANES):
            cur = pbuf[s, pl.ds(POOL_HALO, tm), :]
            acc = cur
            for d in range(1, w):
                acc = acc + pbuf[s, pl.ds(POOL_HALO - d, tm), :]
            halves.append(acc / cnt - cur)
        pooled = jnp.concatenate(halves, axis=-1)
        os_ = slice(g * POOL_GROUP_OUT, (g + 1) * POOL_GROUP_OUT)
        yb = _dot(pooled.astype(BF16), pw_ref[g]) * ps_ref[:, os_]
        ga = z_ref[:, POOL_WIDTH + g * POOL_GROUP_OUT:POOL_WIDTH + (g + 1) * POOL_GROUP_OUT]
        gb = z_ref[:, POOL_WIDTH + D_MODEL + g * POOL_GROUP_OUT:
                   POOL_WIDTH + D_MODEL + (g + 1) * POOL_GROUP_OUT]
        m = ga.astype(F32) * branch_a[:, os_] + gb.astype(F32) * yb
        merged.append(m.astype(BF16))
    pbuf[:, 0:POOL_HALO, :] = pbuf[:, tm:tm + POOL_HALO, :]
    merged = jnp.concatenate(merged, axis=-1)

    x1 = x_ref[...] + _dot(merged, wo_ref[...])
    x1_ref[...] = x1

    v = _rms(x1, ng_ref[...])
    _to_chunked(v_ref, v)
    _zero_pads(v_ref, tm)
    v_hi = v.astype(BF16)
    v_lo = (v - v_hi.astype(F32)).astype(BF16)
    logits = (_dot(v_hi, rwh_ref[...]) + _dot(v_lo, rwh_ref[...]) + _dot(v_hi, rwl_ref[...])
              + rb_ref[...])
    lane = lax.broadcasted_iota(I32, (tm, ROUTER_LANES), 1)
    lane_f = lane.astype(F32)
    neg = jnp.float32(-jnp.inf)
    big = jnp.float32(1e9)

    is_group = jnp.logical_and(lane >= N_EXPERTS, lane < N_EXPERTS + N_GROUPS)
    gl = jnp.where(is_group, logits, neg)
    gmax = jnp.max(gl, axis=-1, keepdims=True)
    gsum = jnp.sum(jnp.exp(gl - gmax), axis=-1, keepdims=True)
    gidx = jnp.min(jnp.where(gl == gmax, lane_f, big), axis=-1, keepdims=True) - N_EXPERTS
    g_val = 1.0 / gsum

    lane_group = jnp.right_shift(lane, 3).astype(F32)
    in_group = jnp.logical_and(lane < N_EXPERTS, lane_group == gidx)
    el = jnp.where(in_group, logits, neg)
    m1 = jnp.max(el, axis=-1, keepdims=True)
    i1 = jnp.min(jnp.where(el == m1, lane_f, big), axis=-1, keepdims=True)
    el2 = jnp.where(lane_f == i1, neg, el)
    m2 = jnp.max(el2, axis=-1, keepdims=True)
    i2 = jnp.min(jnp.where(el2 == m2, lane_f, big), axis=-1, keepdims=True)
    t = jnp.exp(m2 - m1)
    e1 = 1.0 / (1.0 + t)
    e2 = t / (1.0 + t)
    eid_ref[...] = jnp.where(lane == 0, i1, jnp.where(lane == 1, i2, 0.0)).astype(I32)
    rwt_ref[...] = jnp.where(lane == 0, g_val * e1, jnp.where(lane == 1, g_val * e2, 0.0))

    @pl.when(pl.program_id(0) == 0)
    def _():
        cnt_ref[...] = jnp.zeros(cnt_ref.shape, F32)

    picked = jnp.where(jnp.logical_or(lane_f == i1, lane_f == i2), 1.0, 0.0)
    cnt_ref[0:1, :] += jnp.sum(picked, axis=0, keepdims=True)


def _mix_call(x, glu, z, cw, cb, lng, lnb, wco, pw, ps, wo, ng, rwh, rwl, rb, seq_len):
    n = x.shape[0]
    tm = TM_MIX
    zw = z.shape[1]
    const = lambda *shape: pl.BlockSpec(shape, lambda i: (0,) * len(shape))
    row = lambda width: pl.BlockSpec((tm, width), lambda i: (i, 0))
    return pl.pallas_call(
        functools.partial(_mix_kernel, seq_len // tm),
        grid=(n // tm,),
        in_specs=[row(D_MODEL), row(CONV_WIDTH), row(zw),
                  const(CONV_KERNEL, CONV_WIDTH), const(1, CONV_WIDTH), const(1, CONV_WIDTH),
                  const(1, CONV_WIDTH), const(CONV_WIDTH, D_MODEL),
                  const(len(POOL_WINDOWS), POOL_GROUP_IN, POOL_GROUP_OUT), const(1, D_MODEL),
                  const(D_MODEL, D_MODEL), const(1, D_MODEL),
                  const(D_MODEL, ROUTER_LANES), const(D_MODEL, ROUTER_LANES), const(1, ROUTER_LANES)],
        out_specs=[row(D_MODEL), pl.BlockSpec((tm * ROW_PITCH, LANES), lambda i: (i, 0)),
                   row(ROUTER_LANES), row(ROUTER_LANES), const(SUBLANES, ROUTER_LANES)],
        out_shape=[jax.ShapeDtypeStruct((n, D_MODEL), F32),
                   jax.ShapeDtypeStruct((n * ROW_PITCH, LANES), F32),
                   jax.ShapeDtypeStruct((n, ROUTER_LANES), I32),
                   jax.ShapeDtypeStruct((n, ROUTER_LANES), F32),
                   jax.ShapeDtypeStruct((SUBLANES, ROUTER_LANES), F32)],
        scratch_shapes=[pltpu.VMEM((CONV_WIDTH // LANES, CONV_HALO + tm, LANES), F32),
                        pltpu.VMEM((POOL_WIDTH // LANES, POOL_HALO + tm, LANES), F32),
                        pltpu.VMEM((tm, CONV_WIDTH), F32)],
        compiler_params=_params(("arbitrary",)),
        name="mix",
    )(x, glu, z, cw, cb, lng, lnb, wco, pw, ps, wo, ng, rwh, rwl, rb)


def _plan_kernel(eid_ref, offs_ref, dest_ref, carry):
    tp = TM_PLAN

    @pl.when(pl.program_id(0) == 0)
    def _():
        carry[...] = jnp.zeros(carry.shape, F32)

    eid = eid_ref[...]
    lane = lax.broadcasted_iota(I32, (tp, ROUTER_LANES), 1)
    hit0 = lane == eid[:, 0:1]
    hit1 = lane == eid[:, 1:2]
    onehot = jnp.where(jnp.logical_or(hit0, hit1), 1.0, 0.0)
    r = lax.broadcasted_iota(I32, (tp, tp), 0)
    c = lax.broadcasted_iota(I32, (tp, tp), 1)
    tri = jnp.where(c < r, 1.0, 0.0).astype(BF16)
    before = _dot(tri, onehot.astype(BF16)) + carry[0:1, :] + offs_ref[...]
    d0 = jnp.sum(jnp.where(hit0, before, 0.0), axis=-1, keepdims=True)
    d1 = jnp.sum(jnp.where(hit1, before, 0.0), axis=-1, keepdims=True)
    dest_ref[...] = jnp.where(lane == 0, d0, jnp.where(lane == 1, d1, 0.0)).astype(I32)
    carry[0:1, :] += jnp.sum(onehot, axis=0, keepdims=True)


def _plan_call(eid, offs):
    n = eid.shape[0]
    tp = TM_PLAN
    return pl.pallas_call(
        _plan_kernel,
        grid=(n // tp,),
        in_specs=[pl.BlockSpec((tp, ROUTER_LANES), lambda i: (i, 0)),
                  pl.BlockSpec((1, ROUTER_LANES), lambda i: (0, 0))],
        out_specs=pl.BlockSpec((tp, ROUTER_LANES), lambda i: (i, 0)),
        out_shape=jax.ShapeDtypeStruct((n, ROUTER_LANES), I32),
        scratch_shapes=[pltpu.VMEM((SUBLANES, ROUTER_LANES), F32)],
        compiler_params=_params(("arbitrary",)),
        name="plan",
    )(eid, offs)


def _moe_kernel(layer, n_tok,
                dest_ref, cnt_ref,
                v_hbm, wg_hbm, wu_hbm, wd_hbm, out_hbm,
                xbuf, ybuf, xb_s, stg_g, stg_u, stg_d, wgu_s, wd_s,
                row_src, row_dst, offs_s, nxt_s, first_s, cur_s,
                gsem, ssem, wsem):
    tm = TM_MOE
    j = pl.program_id(0)
    n_tiles = pl.num_programs(0)
    slot = j % 2
    row0 = j * tm

    def gather_row(tile, s, r):
        return pltpu.make_async_copy(v_hbm.at[pl.ds(row_src[tile * tm + r], CHUNKS)],
                                     xbuf.at[s, pl.ds(r * ROW_PITCH, CHUNKS)], gsem.at[s])

    def scatter_row(tile, s, r):
        return pltpu.make_async_copy(ybuf.at[s, pl.ds(r * ROW_PITCH, ROW_PITCH)],
                                     out_hbm.at[pl.ds(row_dst[tile * tm + r], ROW_PITCH)], ssem.at[s])

    def start_rows(make, tile, s):
        def body(r, c):
            make(tile, s, r).start()
            return c
        lax.fori_loop(0, tm, body, 0, unroll=16)

    def wait_gather(s):
        pltpu.make_async_copy(v_hbm.at[pl.ds(0, tm * CHUNKS)], xbuf.at[s, pl.ds(0, tm * CHUNKS)],
                              gsem.at[s]).wait()

    def wait_scatter(s):
        pltpu.make_async_copy(ybuf.at[s], out_hbm.at[pl.ds(0, tm * ROW_PITCH)], ssem.at[s]).wait()

    def weight_copies(e):
        w = layer * N_EXPERTS + e
        return [pltpu.make_async_copy(wg_hbm.at[w], stg_g, wsem.at[0]),
                pltpu.make_async_copy(wu_hbm.at[w], stg_u, wsem.at[1]),
                pltpu.make_async_copy(wd_hbm.at[w], stg_d, wsem.at[2])]

    @pl.when(j == 0)
    def _():
        def off_body(e, acc):
            offs_s[e] = acc
            return acc + cnt_ref[e]
        offs_s[N_EXPERTS] = lax.fori_loop(0, N_EXPERTS, off_body, 0)

        def nxt_body(k, nxt):
            e = N_EXPERTS - 1 - k
            nxt_s[e] = nxt
            return jnp.where(cnt_ref[e] > 0, e, nxt)
        first_expert = lax.fori_loop(0, N_EXPERTS, nxt_body, N_EXPERTS)

        def first_body(t, e):
            e = lax.while_loop(lambda e: offs_s[e + 1] <= t * tm, lambda e: e + 1, e)
            first_s[t] = e
            return e
        lax.fori_loop(0, n_tiles, first_body, 0)

        def inv_body(i, c):
            base = i * (4 * ROW_PITCH)
            for k in range(8):
                r = dest_ref[i * 8 + k]
                row_src[r] = base + (k // 2) * ROW_PITCH
                row_dst[r] = base + ((k % 2) * n_tok + k // 2) * ROW_PITCH
            return c
        lax.fori_loop(0, (2 * n_tok) // 8, inv_body, 0)

        for s in range(2):
            _zero_pads(ybuf.at[s], tm)
        cur_s[0] = -1
        for cp in weight_copies(first_expert):
            cp.start()
        start_rows(gather_row, 0, 0)

    wait_gather(slot)

    @pl.when(j + 1 < n_tiles)
    def _():
        start_rows(gather_row, j + 1, 1 - slot)

    xb_s[...] = _from_chunked(xbuf.at[slot], tm).astype(BF16)
    rows = lax.broadcasted_iota(I32, (tm, 1), 0)

    @pl.when(j >= 2)
    def _():
        wait_scatter(slot)

    def expert_step(e):
        @pl.when(cnt_ref[e] > 0)
        def _():
            @pl.when(cur_s[0] != e)
            def _():
                for cp in weight_copies(e):
                    cp.wait()
                wgu_s[:, 0:D_EXPERT] = stg_g[...].astype(BF16)
                wgu_s[:, D_EXPERT:2 * D_EXPERT] = stg_u[...].astype(BF16)
                wd_s[...] = stg_d[...].astype(BF16)
                cur_s[0] = e
                nxt = nxt_s[e]

                @pl.when(nxt < N_EXPERTS)
                def _():
                    for cp in weight_copies(nxt):
                        cp.start()

            lo = offs_s[e] - row0
            hi = offs_s[e + 1] - row0
            gu = _dot(xb_s[...], wgu_s[...])
            gate = gu[:, 0:D_EXPERT]
            h = gate * _sigmoid(gate) * gu[:, D_EXPERT:2 * D_EXPERT]
            h = jnp.where(jnp.logical_and(rows >= lo, rows < hi), h, 0.0)
            y = _dot(h.astype(BF16), wd_s[...])

            @pl.when(e == first_s[j])
            def _():
                _to_chunked(ybuf.at[slot], y)

            @pl.when(e != first_s[j])
            def _():
                _to_chunked(ybuf.at[slot], _from_chunked(ybuf.at[slot], tm) + y)
        return e + 1

    lax.while_loop(lambda e: offs_s[e] < row0 + tm, expert_step, first_s[j])

    start_rows(scatter_row, j, slot)

    @pl.when(j == n_tiles - 1)
    def _():
        wait_scatter(slot)
        wait_scatter(1 - slot)


def _moe_call(v, wg, wu, wd, layer, dest, counts):
    n = v.shape[0] // ROW_PITCH
    tm = TM_MOE
    n_tiles = (2 * n) // tm
    any_spec = pl.BlockSpec(memory_space=pl.ANY)
    return pl.pallas_call(
        functools.partial(_moe_kernel, layer, n),
        grid_spec=pltpu.PrefetchScalarGridSpec(
            num_scalar_prefetch=2,
            grid=(n_tiles,),
            in_specs=[any_spec, any_spec, any_spec, any_spec],
            out_specs=any_spec,
            scratch_shapes=[pltpu.VMEM((2, tm * ROW_PITCH, LANES), F32),
                            pltpu.VMEM((2, tm * ROW_PITCH, LANES), F32),
                            pltpu.VMEM((tm, D_MODEL), BF16),
                            pltpu.VMEM((D_MODEL, D_EXPERT), F32),
                            pltpu.VMEM((D_MODEL, D_EXPERT), F32),
                            pltpu.VMEM((D_EXPERT, D_MODEL), F32),
                            pltpu.VMEM((D_MODEL, 2 * D_EXPERT), BF16),
                            pltpu.VMEM((D_EXPERT, D_MODEL), BF16),
                            pltpu.SMEM((2 * n,), I32),
                            pltpu.SMEM((2 * n,), I32),
                            pltpu.SMEM((N_EXPERTS + 1,), I32),
                            pltpu.SMEM((N_EXPERTS,), I32),
                            pltpu.SMEM((n_tiles,), I32),
                            pltpu.SMEM((1,), I32),
                            pltpu.SemaphoreType.DMA((2,)), pltpu.SemaphoreType.DMA((2,)),
                            pltpu.SemaphoreType.DMA((3,))],
        ),
        out_shape=jax.ShapeDtypeStruct((2 * n * ROW_PITCH, LANES), F32),
        compiler_params=_params(("arbitrary",)),
        name="moe",
    )(dest, counts, v, wg, wu, wd)


def _ple_kernel(last, x1_ref, ya_ref, yb_ref, rwt_ref, p_ref, ng_ref, wg_ref, wp_ref, gn_ref, *out_refs):
    tm = TM_PLE
    rwt = rwt_ref[...]
    x2 = (x1_ref[...] + rwt[:, 0:1] * _from_chunked(ya_ref, tm) + rwt[:, 1:2] * _from_chunked(yb_ref, tm))
    h = _rms(x2, ng_ref[...]).astype(BF16)
    gate = _sigmoid(_dot(h, wg_ref[...]))
    ple = _dot(p_ref[...].astype(BF16), wp_ref[...])
    x3 = x2 + gate * ple
    if last:
        out_refs[0][...] = _rms(x3, gn_ref[...])
    else:
        out_refs[0][...] = x3
        out_refs[1][...] = _rms(x3, gn_ref[...]).astype(BF16)


def _ple_call(x1, y2, rwt, p_all, layer, ng, wg, wp, gn, last):
    n = x1.shape[0]
    tm = TM_PLE
    ple_dim = p_all.shape[-1]
    const = lambda *shape: pl.BlockSpec(shape, lambda i: (0,) * len(shape))
    row = lambda width: pl.BlockSpec((tm, width), lambda i: (i, 0))
    if last:
        out_specs = [row(D_MODEL)]
        out_shape = [jax.ShapeDtypeStruct((n, D_MODEL), F32)]
    else:
        out_specs = [row(D_MODEL), row(D_MODEL)]
        out_shape = [jax.ShapeDtypeStruct((n, D_MODEL), F32), jax.ShapeDtypeStruct((n, D_MODEL), BF16)]
    return pl.pallas_call(
        functools.partial(_ple_kernel, last),
        grid=(n // tm,),
        in_specs=[row(D_MODEL),
                  pl.BlockSpec((tm * ROW_PITCH, LANES), lambda i: (i, 0)),
                  pl.BlockSpec((tm * ROW_PITCH, LANES), lambda i: (i + n // tm, 0)),
                  row(ROUTER_LANES), pl.BlockSpec((None, tm, ple_dim), lambda i: (layer, i, 0)),
                  const(1, D_MODEL), const(D_MODEL, D_MODEL),
                  const(ple_dim, D_MODEL), const(1, D_MODEL)],
        out_specs=out_specs,
        out_shape=out_shape,
        compiler_params=_params(("arbitrary",)),
        name="ple",
    )(x1, y2, y2, rwt, p_all, ng, wg, wp, gn)


def kernel(x, p, norm_mix_g, w_in, b_glu, conv_w, conv_b, conv_ln_g, conv_ln_b, w_conv_out, pool_w, pool_scale, w_out, norm_ffn_g, router_group_w, router_group_b, router_expert_w, router_expert_b, expert_w_gate, expert_w_up, expert_w_down, norm_ple_g, ple_gate_w, ple_proj_w, final_norm_g):
    batch, seq, d = x.shape
    depth = w_in.shape[0]
    n = batch * seq
    assert d == D_MODEL and seq % TM_MIX == 0 and n % TM_IN == 0 and (2 * n) % TM_MOE == 0

    xf = x.reshape(n, d)
    pf = p.reshape(depth, n, p.shape[-1])
    row2 = lambda a: a.reshape(1, -1)
    b_glu3 = b_glu.reshape(depth, 1, -1)
    wg_all = expert_w_gate.reshape(depth * N_EXPERTS, D_MODEL, D_EXPERT)
    wu_all = expert_w_up.reshape(depth * N_EXPERTS, D_MODEL, D_EXPERT)
    wd_all = expert_w_down.reshape(depth * N_EXPERTS, D_EXPERT, D_MODEL)

    u = _norm_call(xf, row2(norm_mix_g[0]))
    out = None
    for l in range(depth):
        glu = _glu_call(u, w_in, b_glu3, l)
        z = _z_call(u, w_in, l)

        pad = ROUTER_LANES - N_EXPERTS - N_GROUPS
        rw = jnp.concatenate([router_expert_w[l], router_group_w[l], jnp.zeros((d, pad), F32)], axis=1)
        rb = jnp.concatenate([router_expert_b[l], router_group_b[l], jnp.zeros((pad,), F32)])
        rw_hi = rw.astype(BF16)
        rw_lo = (rw - rw_hi.astype(F32)).astype(BF16)
        x1, v, eid, rwt, cnt = _mix_call(
            xf, glu, z, conv_w[l], row2(conv_b[l]), row2(conv_ln_g[l]), row2(conv_ln_b[l]),
            w_conv_out[l].astype(BF16), pool_w[l].astype(BF16), row2(pool_scale[l]),
            w_out[l].astype(BF16), row2(norm_ffn_g[l]), rw_hi, rw_lo, row2(rb), seq)

        cnt_row = cnt[0:1, :]
        offs = jnp.cumsum(cnt_row, axis=1) - cnt_row
        dest = _plan_call(eid, offs)[:, :2].reshape(-1)
        y2 = _moe_call(v, wg_all, wu_all, wd_all, l, dest, cnt_row[0, :N_EXPERTS].astype(I32))

        last = l == depth - 1
        gn = final_norm_g if last else norm_mix_g[l + 1]
        res = _ple_call(x1, y2, rwt, pf, l, row2(norm_ple_g[l]), ple_gate_w[l].astype(BF16),
                        ple_proj_w[l].astype(BF16), row2(gn), last)
        if last:
            out = res[0]
        else:
            xf, u = res
    return out.reshape(batch, seq, d)
```

```python
import functools

import jax
import jax.numpy as jnp
from jax import lax
from jax.experimental import pallas as pl
from jax.experimental.pallas import tpu as pltpu

F32 = jnp.float32
BF16 = jnp.bfloat16
I32 = jnp.int32

D_MODEL = 2048
CONV_WIDTH = 1024
CONV_KERNEL = 31
POOL_WIDTH = 1024
POOL_WINDOWS = (2, 4, 8, 16)
POOL_GROUP_IN = 256
POOL_GROUP_OUT = 512
N_GROUPS = 4
EXPERTS_PER_GROUP = 8
N_EXPERTS = N_GROUPS * EXPERTS_PER_GROUP
D_EXPERT = 256
EPS = 1e-6

SUBLANES = 8
LANES = 128
CHUNKS = D_MODEL // LANES
ROW_PITCH = CHUNKS + 4
VMEM_LIMIT = 56 * 1024 * 1024

TM_IN = 1024
TN_IN = 1024
TN_GLU = 512
TM_MIX = 256
TM_PLAN = 512
TM_PLE = 256
TM_MOE = 256
CONV_HALO = 32
POOL_HALO = 16
CONV_ROWS = 64
ROUTER_LANES = 128


def _rms(x, g):
    return x * lax.rsqrt(jnp.mean(x * x, axis=-1, keepdims=True) + EPS) * g


def _sigmoid(x):
    return 1.0 / (1.0 + jnp.exp(-x))


def _dot(a, b):
    return jnp.dot(a, b, preferred_element_type=F32)


def _params(sem):
    return pltpu.CompilerParams(dimension_semantics=sem, vmem_limit_bytes=VMEM_LIMIT)


def _to_chunked(ref, x):
    rows = x.shape[0]
    for c in range(CHUNKS):
        ref[pl.ds(c, rows, stride=ROW_PITCH), :] = x[:, c * LANES:(c + 1) * LANES]


def _zero_pads(ref, rows):
    for c in range(CHUNKS, ROW_PITCH):
        ref[pl.ds(c, rows, stride=ROW_PITCH), :] = jnp.zeros((rows, LANES), F32)


def _from_chunked(ref, rows):
    return jnp.concatenate([ref[pl.ds(c, rows, stride=ROW_PITCH), :] for c in range(CHUNKS)], axis=-1)


def _norm_kernel(x_ref, g_ref, o_ref):
    o_ref[...] = _rms(x_ref[...], g_ref[...]).astype(o_ref.dtype)


def _norm_call(x, g):
    n = x.shape[0]
    tm = 512
    return pl.pallas_call(
        _norm_kernel,
        grid=(n // tm,),
        in_specs=[pl.BlockSpec((tm, D_MODEL), lambda i: (i, 0)),
                  pl.BlockSpec((1, D_MODEL), lambda i: (0, 0))],
        out_specs=pl.BlockSpec((tm, D_MODEL), lambda i: (i, 0)),
        out_shape=jax.ShapeDtypeStruct((n, D_MODEL), BF16),
        compiler_params=_params(("arbitrary",)),
        name="norm",
    )(x, g)


def _glu_kernel(u_ref, w1_ref, w2_ref, b1_ref, b2_ref, o_ref, w1s, w2s):
    @pl.when(pl.program_id(1) == 0)
    def _():
        w1s[...] = w1_ref[...].astype(BF16)
        w2s[...] = w2_ref[...].astype(BF16)

    u = u_ref[...]
    a = _dot(u, w1s[...]) + b1_ref[...]
    g = _dot(u, w2s[...]) + b2_ref[...]
    o_ref[...] = (a * _sigmoid(g)).astype(o_ref.dtype)


def _glu_call(u, w_in, b_glu, layer):
    n = u.shape[0]
    nj = CONV_WIDTH // TN_GLU
    return pl.pallas_call(
        _glu_kernel,
        grid=(nj, n // TM_IN),
        in_specs=[pl.BlockSpec((TM_IN, D_MODEL), lambda j, i: (i, 0)),
                  pl.BlockSpec((None, D_MODEL, TN_GLU), lambda j, i: (layer, 0, j)),
                  pl.BlockSpec((None, D_MODEL, TN_GLU), lambda j, i: (layer, 0, j + nj)),
                  pl.BlockSpec((None, 1, TN_GLU), lambda j, i: (layer, 0, j)),
                  pl.BlockSpec((None, 1, TN_GLU), lambda j, i: (layer, 0, j + nj))],
        out_specs=pl.BlockSpec((TM_IN, TN_GLU), lambda j, i: (i, j)),
        out_shape=jax.ShapeDtypeStruct((n, CONV_WIDTH), BF16),
        scratch_shapes=[pltpu.VMEM((D_MODEL, TN_GLU), BF16), pltpu.VMEM((D_MODEL, TN_GLU), BF16)],
        compiler_params=_params(("arbitrary", "arbitrary")),
        name="glu",
    )(u, w_in, w_in, b_glu, b_glu)


def _z_kernel(u_ref, w_ref, o_ref, ws):
    j = pl.program_id(0)

    @pl.when(pl.program_id(1) == 0)
    def _():
        ws[...] = w_ref[...].astype(BF16)

    z = _dot(u_ref[...], ws[...])

    @pl.when(j == 0)
    def _():
        o_ref[...] = z.astype(o_ref.dtype)

    @pl.when(j > 0)
    def _():
        o_ref[...] = _sigmoid(z).astype(o_ref.dtype)


def _z_call(u, w_in, layer):
    n = u.shape[0]
    width = POOL_WIDTH + 2 * D_MODEL
    col0 = (2 * CONV_WIDTH) // TN_IN
    return pl.pallas_call(
        _z_kernel,
        grid=(width // TN_IN, n // TM_IN),
        in_specs=[pl.BlockSpec((TM_IN, D_MODEL), lambda j, i: (i, 0)),
                  pl.BlockSpec((None, D_MODEL, TN_IN), lambda j, i: (layer, 0, j + col0))],
        out_specs=pl.BlockSpec((TM_IN, TN_IN), lambda j, i: (i, j)),
        out_shape=jax.ShapeDtypeStruct((n, width), BF16),
        scratch_shapes=[pltpu.VMEM((D_MODEL, TN_IN), BF16)],
        compiler_params=_params(("arbitrary", "arbitrary")),
        name="zproj",
    )(u, w_in)


def _conv_taps(r):
    lead = CONV_HALO - (CONV_KERNEL - 1)
    return [(q, SUBLANES * q + r - lead) for q in range((lead + CONV_KERNEL) // SUBLANES + 1)
            if 0 <= SUBLANES * q + r - lead < CONV_KERNEL]


def _mix_kernel(tiles_per_seq,
                x_ref, glu_ref, z_ref, cw_ref, cb_ref, lng_ref, lnb_ref, wco_ref, pw_ref,
                ps_ref, wo_ref, ng_ref, rwh_ref, rwl_ref, rb_ref,
                x1_ref, v_ref, eid_ref, rwt_ref, cnt_ref,
                cbuf, pbuf, conv_s):
    tm = TM_MIX
    seq_tile = pl.program_id(0) % tiles_per_seq

    @pl.when(seq_tile == 0)
    def _():
        cbuf[:, 0:CONV_HALO, :] = jnp.zeros((CONV_WIDTH // LANES, CONV_HALO, LANES), F32)
        pbuf[:, 0:POOL_HALO, :] = jnp.zeros((POOL_WIDTH // LANES, POOL_HALO, LANES), F32)

    for c in range(CONV_WIDTH // LANES):
        cbuf[c, CONV_HALO:CONV_HALO + tm, :] = glu_ref[:, c * LANES:(c + 1) * LANES].astype(F32)
    for c in range(POOL_WIDTH // LANES):
        pbuf[c, POOL_HALO:POOL_HALO + tm, :] = z_ref[:, c * LANES:(c + 1) * LANES].astype(F32)

    for c in range(CONV_WIDTH // LANES):
        ls = slice(c * LANES, (c + 1) * LANES)
        for rb in range(tm // CONV_ROWS):
            acc = None
            for r in range(SUBLANES):
                taps = _conv_taps(r)
                qmin, qmax = taps[0][0], taps[-1][0]
                rows = CONV_ROWS + SUBLANES * (qmax - qmin)
                a = cbuf[c, pl.ds(rb * CONV_ROWS + SUBLANES * qmin + r, rows), :]
                for q, k in taps:
                    o = SUBLANES * (q - qmin)
                    term = a[o:o + CONV_ROWS, :] * cw_ref[k:k + 1, ls]
                    acc = term if acc is None else acc + term
            conv_s[rb * CONV_ROWS:(rb + 1) * CONV_ROWS, ls] = acc + cb_ref[:, ls]
    cbuf[:, 0:CONV_HALO, :] = cbuf[:, tm:tm + CONV_HALO, :]

    a = conv_s[...]
    mu = jnp.mean(a, axis=-1, keepdims=True)
    ac = a - mu
    var = jnp.mean(ac * ac, axis=-1, keepdims=True)
    a = ac * lax.rsqrt(var + EPS) * lng_ref[...] + lnb_ref[...]
    a = a * _sigmoid(a)
    branch_a = _dot(a.astype(BF16), wco_ref[...])

    pos1 = seq_tile * tm + lax.broadcasted_iota(I32, (tm, 1), 0) + 1
    merged = []
    for g, w in enumerate(POOL_WINDOWS):
        cnt = jnp.minimum(pos1, w).astype(F32)
        halves = []
        for s in range(g * POOL_GROUP_IN // LANES, (g + 1) * POOL_GROUP_IN // LANES):
            cur = pbuf[s, pl.ds(POOL_HALO, tm), :]
            acc = cur
            for d in range(1, w):
                acc = acc + pbuf[s, pl.ds(POOL_HALO - d, tm), :]
            halves.append(acc / cnt - cur)
        pooled = jnp.concatenate(halves, axis=-1)
        os_ = slice(g * POOL_GROUP_OUT, (g + 1) * POOL_GROUP_OUT)
        yb = _dot(pooled.astype(BF16), pw_ref[g]) * ps_ref[:, os_]
        ga = z_ref[:, POOL_WIDTH + g * POOL_GROUP_OUT:POOL_WIDTH + (g + 1) * POOL_GROUP_OUT]
        gb = z_ref[:, POOL_WIDTH + D_MODEL + g * POOL_GROUP_OUT:
                   POOL_WIDTH + D_MODEL + (g + 1) * POOL_GROUP_OUT]
        m = ga.astype(F32) * branch_a[:, os_] + gb.astype(F32) * yb
        merged.append(m.astype(BF16))
    pbuf[:, 0:POOL_HALO, :] = pbuf[:, tm:tm + POOL_HALO, :]
    merged = jnp.concatenate(merged, axis=-1)

    x1 = x_ref[...] + _dot(merged, wo_ref[...])
    x1_ref[...] = x1

    v = _rms(x1, ng_ref[...])
    _to_chunked(v_ref, v)
    _zero_pads(v_ref, tm)
    v_hi = v.astype(BF16)
    v_lo = (v - v_hi.astype(F32)).astype(BF16)
    logits = (_dot(v_hi, rwh_ref[...]) + _dot(v_lo, rwh_ref[...]) + _dot(v_hi, rwl_ref[...])
              + rb_ref[...])
    lane = lax.broadcasted_iota(I32, (tm, ROUTER_LANES), 1)
    lane_f = lane.astype(F32)
    neg = jnp.float32(-jnp.inf)
    big = jnp.float32(1e9)

    is_group = jnp.logical_and(lane >= N_EXPERTS, lane < N_EXPERTS + N_GROUPS)
    gl = jnp.where(is_group, logits, neg)
    gmax = jnp.max(gl, axis=-1, keepdims=True)
    gsum = jnp.sum(jnp.exp(gl - gmax), axis=-1, keepdims=True)
    gidx = jnp.min(jnp.where(gl == gmax, lane_f, big), axis=-1, keepdims=True) - N_EXPERTS
    g_val = 1.0 / gsum

    lane_group = jnp.right_shift(lane, 3).astype(F32)
    in_group = jnp.logical_and(lane < N_EXPERTS, lane_group == gidx)
    el = jnp.where(in_group, logits, neg)
    m1 = jnp.max(el, axis=-1, keepdims=True)
    i1 = jnp.min(jnp.where(el == m1, lane_f, big), axis=-1, keepdims=True)
    el2 = jnp.where(lane_f == i1, neg, el)
    m2 = jnp.max(el2, axis=-1, keepdims=True)
    i2 = jnp.min(jnp.where(el2 == m2, lane_f, big), axis=-1, keepdims=True)
    t = jnp.exp(m2 - m1)
    e1 = 1.0 / (1.0 + t)
    e2 = t / (1.0 + t)
    eid_ref[...] = jnp.where(lane == 0, i1, jnp.where(lane == 1, i2, 0.0)).astype(I32)
    rwt_ref[...] = jnp.where(lane == 0, g_val * e1, jnp.where(lane == 1, g_val * e2, 0.0))

    @pl.when(pl.program_id(0) == 0)
    def _():
        cnt_ref[...] = jnp.zeros(cnt_ref.shape, F32)

    picked = jnp.where(jnp.logical_or(lane_f == i1, lane_f == i2), 1.0, 0.0)
    cnt_ref[0:1, :] += jnp.sum(picked, axis=0, keepdims=True)


def _mix_call(x, glu, z, cw, cb, lng, lnb, wco, pw, ps, wo, ng, rwh, rwl, rb, seq_len):
    n = x.shape[0]
    tm = TM_MIX
    zw = z.shape[1]
    const = lambda *shape: pl.BlockSpec(shape, lambda i: (0,) * len(shape))
    row = lambda width: pl.BlockSpec((tm, width), lambda i: (i, 0))
    return pl.pallas_call(
        functools.partial(_mix_kernel, seq_len // tm),
        grid=(n // tm,),
        in_specs=[row(D_MODEL), row(CONV_WIDTH), row(zw),
                  const(CONV_KERNEL, CONV_WIDTH), const(1, CONV_WIDTH), const(1, CONV_WIDTH),
                  const(1, CONV_WIDTH), const(CONV_WIDTH, D_MODEL),
                  const(len(POOL_WINDOWS), POOL_GROUP_IN, POOL_GROUP_OUT), const(1, D_MODEL),
                  const(D_MODEL, D_MODEL), const(1, D_MODEL),
                  const(D_MODEL, ROUTER_LANES), const(D_MODEL, ROUTER_LANES), const(1, ROUTER_LANES)],
        out_specs=[row(D_MODEL), pl.BlockSpec((tm * ROW_PITCH, LANES), lambda i: (i, 0)),
                   row(ROUTER_LANES), row(ROUTER_LANES), const(SUBLANES, ROUTER_LANES)],
        out_shape=[jax.ShapeDtypeStruct((n, D_MODEL), F32),
                   jax.ShapeDtypeStruct((n * ROW_PITCH, LANES), F32),
                   jax.ShapeDtypeStruct((n, ROUTER_LANES), I32),
                   jax.ShapeDtypeStruct((n, ROUTER_LANES), F32),
                   jax.ShapeDtypeStruct((SUBLANES, ROUTER_LANES), F32)],
        scratch_shapes=[pltpu.VMEM((CONV_WIDTH // LANES, CONV_HALO + tm, LANES), F32),
                        pltpu.VMEM((POOL_WIDTH // LANES, POOL_HALO + tm, LANES), F32),
                        pltpu.VMEM((tm, CONV_WIDTH), F32)],
        compiler_params=_params(("arbitrary",)),
        name="mix",
    )(x, glu, z, cw, cb, lng, lnb, wco, pw, ps, wo, ng, rwh, rwl, rb)


def _plan_kernel(eid_ref, offs_ref, dest_ref, carry):
    tp = TM_PLAN

    @pl.when(pl.program_id(0) == 0)
    def _():
        carry[...] = jnp.zeros(carry.shape, F32)

    eid = eid_ref[...]
    lane = lax.broadcasted_iota(I32, (tp, ROUTER_LANES), 1)
    hit0 = lane == eid[:, 0:1]
    hit1 = lane == eid[:, 1:2]
    onehot = jnp.where(jnp.logical_or(hit0, hit1), 1.0, 0.0)
    r = lax.broadcasted_iota(I32, (tp, tp), 0)
    c = lax.broadcasted_iota(I32, (tp, tp), 1)
    tri = jnp.where(c < r, 1.0, 0.0).astype(BF16)
    before = _dot(tri, onehot.astype(BF16)) + carry[0:1, :] + offs_ref[...]
    d0 = jnp.sum(jnp.where(hit0, before, 0.0), axis=-1, keepdims=True)
    d1 = jnp.sum(jnp.where(hit1, before, 0.0), axis=-1, keepdims=True)
    dest_ref[...] = jnp.where(lane == 0, d0, jnp.where(lane == 1, d1, 0.0)).astype(I32)
    carry[0:1, :] += jnp.sum(onehot, axis=0, keepdims=True)


def _plan_call(eid, offs):
    n = eid.shape[0]
    tp = TM_PLAN
    return pl.pallas_call(
        _plan_kernel,
        grid=(n // tp,),
        in_specs=[pl.BlockSpec((tp, ROUTER_LANES), lambda i: (i, 0)),
                  pl.BlockSpec((1, ROUTER_LANES), lambda i: (0, 0))],
        out_specs=pl.BlockSpec((tp, ROUTER_LANES), lambda i: (i, 0)),
        out_shape=jax.ShapeDtypeStruct((n, ROUTER_LANES), I32),
        scratch_shapes=[pltpu.VMEM((SUBLANES, ROUTER_LANES), F32)],
        compiler_params=_params(("arbitrary",)),
        name="plan",
    )(eid, offs)


def _moe_kernel(layer, n_tok,
                dest_ref, cnt_ref,
                v_hbm, wg_hbm, wu_hbm, wd_hbm, out_hbm,
                xbuf, ybuf, xb_s, stg_g, stg_u, stg_d, wgu_s, wd_s,
                row_src, row_dst, offs_s, nxt_s, first_s, cur_s,
                gsem, ssem, wsem):
    tm = TM_MOE
    j = pl.program_id(0)
    n_tiles = pl.num_programs(0)
    slot = j % 2
    row0 = j * tm

    def gather_row(tile, s, r):
        return pltpu.make_async_copy(v_hbm.at[pl.ds(row_src[tile * tm + r], CHUNKS)],
                                     xbuf.at[s, pl.ds(r * ROW_PITCH, CHUNKS)], gsem.at[s])

    def scatter_row(tile, s, r):
        return pltpu.make_async_copy(ybuf.at[s, pl.ds(r * ROW_PITCH, ROW_PITCH)],
                                     out_hbm.at[pl.ds(row_dst[tile * tm + r], ROW_PITCH)], ssem.at[s])

    def start_rows(make, tile, s):
        def body(r, c):
            make(tile, s, r).start()
            return c
        lax.fori_loop(0, tm, body, 0, unroll=32)

    def wait_gather(s):
        pltpu.make_async_copy(v_hbm.at[pl.ds(0, tm * CHUNKS)], xbuf.at[s, pl.ds(0, tm * CHUNKS)],
                              gsem.at[s]).wait()

    def wait_scatter(s):
        pltpu.make_async_copy(ybuf.at[s], out_hbm.at[pl.ds(0, tm * ROW_PITCH)], ssem.at[s]).wait()

    def weight_copies(e):
        w = layer * N_EXPERTS + e
        return [pltpu.make_async_copy(wg_hbm.at[w], stg_g, wsem.at[0]),
                pltpu.make_async_copy(wu_hbm.at[w], stg_u, wsem.at[1]),
                pltpu.make_async_copy(wd_hbm.at[w], stg_d, wsem.at[2])]

    @pl.when(j == 0)
    def _():
        def off_body(e, acc):
            offs_s[e] = acc
            return acc + cnt_ref[e]
        offs_s[N_EXPERTS] = lax.fori_loop(0, N_EXPERTS, off_body, 0)

        def nxt_body(k, nxt):
            e = N_EXPERTS - 1 - k
            nxt_s[e] = nxt
            return jnp.where(cnt_ref[e] > 0, e, nxt)
        first_expert = lax.fori_loop(0, N_EXPERTS, nxt_body, N_EXPERTS)

        def first_body(t, e):
            e = lax.while_loop(lambda e: offs_s[e + 1] <= t * tm, lambda e: e + 1, e)
            first_s[t] = e
            return e
        lax.fori_loop(0, n_tiles, first_body, 0)

        def inv_body(i, c):
            base = i * (8 * ROW_PITCH)
            for k in range(16):
                r = dest_ref[i * 16 + k]
                row_src[r] = base + (k // 2) * ROW_PITCH
                row_dst[r] = base + ((k % 2) * n_tok + k // 2) * ROW_PITCH
            return c
        lax.fori_loop(0, (2 * n_tok) // 16, inv_body, 0)

        for s in range(2):
            _zero_pads(ybuf.at[s], tm)
        cur_s[0] = -1
        for cp in weight_copies(first_expert):
            cp.start()
        start_rows(gather_row, 0, 0)

    wait_gather(slot)

    @pl.when(j + 1 < n_tiles)
    def _():
        start_rows(gather_row, j + 1, 1 - slot)

    xb_s[...] = _from_chunked(xbuf.at[slot], tm).astype(BF16)
    rows = lax.broadcasted_iota(I32, (tm, 1), 0)

    @pl.when(j >= 2)
    def _():
        wait_scatter(slot)

    def expert_step(e):
        @pl.when(cnt_ref[e] > 0)
        def _():
            @pl.when(cur_s[0] != e)
            def _():
                for cp in weight_copies(e):
                    cp.wait()
                wgu_s[:, 0:D_EXPERT] = stg_g[...].astype(BF16)
                wgu_s[:, D_EXPERT:2 * D_EXPERT] = stg_u[...].astype(BF16)
                wd_s[...] = stg_d[...].astype(BF16)
                cur_s[0] = e
                nxt = nxt_s[e]

                @pl.when(nxt < N_EXPERTS)
                def _():
                    for cp in weight_copies(nxt):
                        cp.start()

            lo = offs_s[e] - row0
            hi = offs_s[e + 1] - row0
            gu = _dot(xb_s[...], wgu_s[...])
            gate = gu[:, 0:D_EXPERT]
            h = gate * _sigmoid(gate) * gu[:, D_EXPERT:2 * D_EXPERT]
            h = jnp.where(jnp.logical_and(rows >= lo, rows < hi), h, 0.0)
            y = _dot(h.astype(BF16), wd_s[...])

            @pl.when(e == first_s[j])
            def _():
                _to_chunked(ybuf.at[slot], y)

            @pl.when(e != first_s[j])
            def _():
                _to_chunked(ybuf.at[slot], _from_chunked(ybuf.at[slot], tm) + y)
        return e + 1

    lax.while_loop(lambda e: offs_s[e] < row0 + tm, expert_step, first_s[j])

    start_rows(scatter_row, j, slot)

    @pl.when(j == n_tiles - 1)
    def _():
        wait_scatter(slot)
        wait_scatter(1 - slot)


def _moe_call(v, wg, wu, wd, layer, dest, counts):
    n = v.shape[0] // ROW_PITCH
    tm = TM_MOE
    n_tiles = (2 * n) // tm
    any_spec = pl.BlockSpec(memory_space=pl.ANY)
    return pl.pallas_call(
        functools.partial(_moe_kernel, layer, n),
        grid_spec=pltpu.PrefetchScalarGridSpec(
            num_scalar_prefetch=2,
            grid=(n_tiles,),
            in_specs=[any_spec, any_spec, any_spec, any_spec],
            out_specs=any_spec,
            scratch_shapes=[pltpu.VMEM((2, tm * ROW_PITCH, LANES), F32),
                            pltpu.VMEM((2, tm * ROW_PITCH, LANES), F32),
                            pltpu.VMEM((tm, D_MODEL), BF16),
                            pltpu.VMEM((D_MODEL, D_EXPERT), F32),
                            pltpu.VMEM((D_MODEL, D_EXPERT), F32),
                            pltpu.VMEM((D_EXPERT, D_MODEL), F32),
                            pltpu.VMEM((D_MODEL, 2 * D_EXPERT), BF16),
                            pltpu.VMEM((D_EXPERT, D_MODEL), BF16),
                            pltpu.SMEM((2 * n,), I32),
                            pltpu.SMEM((2 * n,), I32),
                            pltpu.SMEM((N_EXPERTS + 1,), I32),
                            pltpu.SMEM((N_EXPERTS,), I32),
                            pltpu.SMEM((n_tiles,), I32),
                            pltpu.SMEM((1,), I32),
                            pltpu.SemaphoreType.DMA((2,)), pltpu.SemaphoreType.DMA((2,)),
                            pltpu.SemaphoreType.DMA((3,))],
        ),
        out_shape=jax.ShapeDtypeStruct((2 * n * ROW_PITCH, LANES), F32),
        compiler_params=_params(("arbitrary",)),
        name="moe",
    )(dest, counts, v, wg, wu, wd)


def _ple_kernel(last, x1_ref, ya_ref, yb_ref, rwt_ref, p_ref, ng_ref, wg_ref, wp_ref, gn_ref, *out_refs):
    tm = TM_PLE
    rwt = rwt_ref[...]
    x2 = (x1_ref[...] + rwt[:, 0:1] * _from_chunked(ya_ref, tm) + rwt[:, 1:2] * _from_chunked(yb_ref, tm))
    h = _rms(x2, ng_ref[...]).astype(BF16)
    gate = _sigmoid(_dot(h, wg_ref[...]))
    ple = _dot(p_ref[...].astype(BF16), wp_ref[...])
    x3 = x2 + gate * ple
    if last:
        out_refs[0][...] = _rms(x3, gn_ref[...])
    else:
        out_refs[0][...] = x3
        out_refs[1][...] = _rms(x3, gn_ref[...]).astype(BF16)


def _ple_call(x1, y2, rwt, p_all, layer, ng, wg, wp, gn, last):
    n = x1.shape[0]
    tm = TM_PLE
    ple_dim = p_all.shape[-1]
    const = lambda *shape: pl.BlockSpec(shape, lambda i: (0,) * len(shape))
    row = lambda width: pl.BlockSpec((tm, width), lambda i: (i, 0))
    if last:
        out_specs = [row(D_MODEL)]
        out_shape = [jax.ShapeDtypeStruct((n, D_MODEL), F32)]
    else:
        out_specs = [row(D_MODEL), row(D_MODEL)]
        out_shape = [jax.ShapeDtypeStruct((n, D_MODEL), F32), jax.ShapeDtypeStruct((n, D_MODEL), BF16)]
    return pl.pallas_call(
        functools.partial(_ple_kernel, last),
        grid=(n // tm,),
        in_specs=[row(D_MODEL),
                  pl.BlockSpec((tm * ROW_PITCH, LANES), lambda i: (i, 0)),
                  pl.BlockSpec((tm * ROW_PITCH, LANES), lambda i: (i + n // tm, 0)),
                  row(ROUTER_LANES), pl.BlockSpec((None, tm, ple_dim), lambda i: (layer, i, 0)),
                  const(1, D_MODEL), const(D_MODEL, D_MODEL),
                  const(ple_dim, D_MODEL), const(1, D_MODEL)],
        out_specs=out_specs,
        out_shape=out_shape,
        compiler_params=_params(("arbitrary",)),
        name="ple",
    )(x1, y2, y2, rwt, p_all, ng, wg, wp, gn)


def kernel(x, p, norm_mix_g, w_in, b_glu, conv_w, conv_b, conv_ln_g, conv_ln_b, w_conv_out, pool_w, pool_scale, w_out, norm_ffn_g, router_group_w, router_group_b, router_expert_w, router_expert_b, expert_w_gate, expert_w_up, expert_w_down, norm_ple_g, ple_gate_w, ple_proj_w, final_norm_g):
    batch, seq, d = x.shape
    depth = w_in.shape[0]
    n = batch * seq
    assert d == D_MODEL and seq % TM_MIX == 0 and n % TM_IN == 0 and (2 * n) % TM_MOE == 0

    xf = x.reshape(n, d)
    pf = p.reshape(depth, n, p.shape[-1])
    row2 = lambda a: a.reshape(1, -1)
    b_glu3 = b_glu.reshape(depth, 1, -1)
    wg_all = expert_w_gate.reshape(depth * N_EXPERTS, D_MODEL, D_EXPERT)
    wu_all = expert_w_up.reshape(depth * N_EXPERTS, D_MODEL, D_EXPERT)
    wd_all = expert_w_down.reshape(depth * N_EXPERTS, D_EXPERT, D_MODEL)

    u = _norm_call(xf, row2(norm_mix_g[0]))
    out = None
    for l in range(depth):
        glu = _glu_call(u, w_in, b_glu3, l)
        z = _z_call(u, w_in, l)

        pad = ROUTER_LANES - N_EXPERTS - N_GROUPS
        rw = jnp.concatenate([router_expert_w[l], router_group_w[l], jnp.zeros((d, pad), F32)], axis=1)
        rb = jnp.concatenate([router_expert_b[l], router_group_b[l], jnp.zeros((pad,), F32)])
        rw_hi = rw.astype(BF16)
        rw_lo = (rw - rw_hi.astype(F32)).astype(BF16)
        x1, v, eid, rwt, cnt = _mix_call(
            xf, glu, z, conv_w[l], row2(conv_b[l]), row2(conv_ln_g[l]), row2(conv_ln_b[l]),
            w_conv_out[l].astype(BF16), pool_w[l].astype(BF16), row2(pool_scale[l]),
            w_out[l].astype(BF16), row2(norm_ffn_g[l]), rw_hi, rw_lo, row2(rb), seq)

        cnt_row = cnt[0:1, :]
        offs = jnp.cumsum(cnt_row, axis=1) - cnt_row
        dest = _plan_call(eid, offs)[:, :2].reshape(-1)
        y2 = _moe_call(v, wg_all, wu_all, wd_all, l, dest, cnt_row[0, :N_EXPERTS].astype(I32))

        last = l == depth - 1
        gn = final_norm_g if last else norm_mix_g[l + 1]
        res = _ple_call(x1, y2, rwt, pf, l, row2(norm_ple_g[l]), ple_gate_w[l].astype(BF16),
                        ple_proj_w[l].astype(BF16), row2(gn), last)
        if last:
            out = res[0]
        else:
            xf, u = res
    return out.reshape(batch, seq, d)
```

```python
import functools

import jax
import jax.numpy as jnp
from jax import lax
from jax.experimental import pallas as pl
from jax.experimental.pallas import tpu as pltpu

F32 = jnp.float32
BF16 = jnp.bfloat16
I32 = jnp.int32

D_MODEL = 2048
CONV_WIDTH = 1024
CONV_KERNEL = 31
POOL_WIDTH = 1024
POOL_WINDOWS = (2, 4, 8, 16)
POOL_GROUP_IN = 256
POOL_GROUP_OUT = 512
N_GROUPS = 4
EXPERTS_PER_GROUP = 8
N_EXPERTS = N_GROUPS * EXPERTS_PER_GROUP
D_EXPERT = 256
EPS = 1e-6

SUBLANES = 8
LANES = 128
CHUNKS = D_MODEL // LANES
ROW_PITCH = CHUNKS + 4
VMEM_LIMIT = 56 * 1024 * 1024

TM_IN = 1024
TN_IN = 1024
TN_GLU = 512
TM_MIX = 256
TM_PLAN = 512
TM_PLE = 256
TM_MOE = 256
CONV_HALO = 32
POOL_HALO = 16
CONV_ROWS = 64
ROUTER_LANES = 128


def _rms(x, g):
    return x * lax.rsqrt(jnp.mean(x * x, axis=-1, keepdims=True) + EPS) * g


def _sigmoid(x):
    return 1.0 / (1.0 + jnp.exp(-x))


def _dot(a, b):
    return jnp.dot(a, b, preferred_element_type=F32)


def _params(sem):
    return pltpu.CompilerParams(dimension_semantics=sem, vmem_limit_bytes=VMEM_LIMIT)


def _to_chunked(ref, x):
    rows = x.shape[0]
    for c in range(CHUNKS):
        ref[pl.ds(c, rows, stride=ROW_PITCH), :] = x[:, c * LANES:(c + 1) * LANES]


def _zero_pads(ref, rows):
    for c in range(CHUNKS, ROW_PITCH):
        ref[pl.ds(c, rows, stride=ROW_PITCH), :] = jnp.zeros((rows, LANES), F32)


def _from_chunked(ref, rows):
    return jnp.concatenate([ref[pl.ds(c, rows, stride=ROW_PITCH), :] for c in range(CHUNKS)], axis=-1)


def _norm_kernel(x_ref, g_ref, o_ref):
    o_ref[...] = _rms(x_ref[...], g_ref[...]).astype(o_ref.dtype)


def _norm_call(x, g):
    n = x.shape[0]
    tm = 512
    return pl.pallas_call(
        _norm_kernel,
        grid=(n // tm,),
        in_specs=[pl.BlockSpec((tm, D_MODEL), lambda i: (i, 0)),
                  pl.BlockSpec((1, D_MODEL), lambda i: (0, 0))],
        out_specs=pl.BlockSpec((tm, D_MODEL), lambda i: (i, 0)),
        out_shape=jax.ShapeDtypeStruct((n, D_MODEL), BF16),
        compiler_params=_params(("arbitrary",)),
        name="norm",
    )(x, g)


def _glu_kernel(u_ref, w1_ref, w2_ref, b1_ref, b2_ref, o_ref, w1s, w2s):
    @pl.when(pl.program_id(1) == 0)
    def _():
        w1s[...] = w1_ref[...].astype(BF16)
        w2s[...] = w2_ref[...].astype(BF16)

    u = u_ref[...]
    a = _dot(u, w1s[...]) + b1_ref[...]
    g = _dot(u, w2s[...]) + b2_ref[...]
    o_ref[...] = (a * _sigmoid(g)).astype(o_ref.dtype)


def _glu_call(u, w_in, b_glu, layer):
    n = u.shape[0]
    nj = CONV_WIDTH // TN_GLU
    return pl.pallas_call(
        _glu_kernel,
        grid=(nj, n // TM_IN),
        in_specs=[pl.BlockSpec((TM_IN, D_MODEL), lambda j, i: (i, 0)),
                  pl.BlockSpec((None, D_MODEL, TN_GLU), lambda j, i: (layer, 0, j)),
                  pl.BlockSpec((None, D_MODEL, TN_GLU), lambda j, i: (layer, 0, j + nj)),
                  pl.BlockSpec((None, 1, TN_GLU), lambda j, i: (layer, 0, j)),
                  pl.BlockSpec((None, 1, TN_GLU), lambda j, i: (layer, 0, j + nj))],
        out_specs=pl.BlockSpec((TM_IN, TN_GLU), lambda j, i: (i, j)),
        out_shape=jax.ShapeDtypeStruct((n, CONV_WIDTH), BF16),
        scratch_shapes=[pltpu.VMEM((D_MODEL, TN_GLU), BF16), pltpu.VMEM((D_MODEL, TN_GLU), BF16)],
        compiler_params=_params(("arbitrary", "arbitrary")),
        name="glu",
    )(u, w_in, w_in, b_glu, b_glu)


def _z_kernel(u_ref, w_ref, o_ref, ws):
    j = pl.program_id(0)

    @pl.when(pl.program_id(1) == 0)
    def _():
        ws[...] = w_ref[...].astype(BF16)

    z = _dot(u_ref[...], ws[...])

    @pl.when(j == 0)
    def _():
        o_ref[...] = z.astype(o_ref.dtype)

    @pl.when(j > 0)
    def _():
        o_ref[...] = _sigmoid(z).astype(o_ref.dtype)


def _z_call(u, w_in, layer):
    n = u.shape[0]
    width = POOL_WIDTH + 2 * D_MODEL
    col0 = (2 * CONV_WIDTH) // TN_IN
    return pl.pallas_call(
        _z_kernel,
        grid=(width // TN_IN, n // TM_IN),
        in_specs=[pl.BlockSpec((TM_IN, D_MODEL), lambda j, i: (i, 0)),
                  pl.BlockSpec((None, D_MODEL, TN_IN), lambda j, i: (layer, 0, j + col0))],
        out_specs=pl.BlockSpec((TM_IN, TN_IN), lambda j, i: (i, j)),
        out_shape=jax.ShapeDtypeStruct((n, width), BF16),
        scratch_shapes=[pltpu.VMEM((D_MODEL, TN_IN), BF16)],
        compiler_params=_params(("arbitrary", "arbitrary")),
        name="zproj",
    )(u, w_in)


def _conv_taps(r):
    lead = CONV_HALO - (CONV_KERNEL - 1)
    return [(q, SUBLANES * q + r - lead) for q in range((lead + CONV_KERNEL) // SUBLANES + 1)
            if 0 <= SUBLANES * q + r - lead < CONV_KERNEL]


def _mix_kernel(tiles_per_seq,
                x_ref, glu_ref, z_ref, cw_ref, cb_ref, lng_ref, lnb_ref, wco_ref, pw_ref,
                ps_ref, wo_ref, ng_ref, rwh_ref, rwl_ref, rb_ref,
                x1_ref, v_ref, eid_ref, rwt_ref, cnt_ref,
                cbuf, pbuf, conv_s):
    tm = TM_MIX
    seq_tile = pl.program_id(0) % tiles_per_seq

    @pl.when(seq_tile == 0)
    def _():
        cbuf[:, 0:CONV_HALO, :] = jnp.zeros((CONV_WIDTH // LANES, CONV_HALO, LANES), F32)
        pbuf[:, 0:POOL_HALO, :] = jnp.zeros((POOL_WIDTH // LANES, POOL_HALO, LANES), F32)

    for c in range(CONV_WIDTH // LANES):
        cbuf[c, CONV_HALO:CONV_HALO + tm, :] = glu_ref[:, c * LANES:(c + 1) * LANES].astype(F32)
    for c in range(POOL_WIDTH // LANES):
        pbuf[c, POOL_HALO:POOL_HALO + tm, :] = z_ref[:, c * LANES:(c + 1) * LANES].astype(F32)

    for c in range(CONV_WIDTH // LANES):
        ls = slice(c * LANES, (c + 1) * LANES)
        for rb in range(tm // CONV_ROWS):
            acc = None
            for r in range(SUBLANES):
                taps = _conv_taps(r)
                qmin, qmax = taps[0][0], taps[-1][0]
                rows = CONV_ROWS + SUBLANES * (qmax - qmin)
                a = cbuf[c, pl.ds(rb * CONV_ROWS + SUBLANES * qmin + r, rows), :]
                for q, k in taps:
                    o = SUBLANES * (q - qmin)
                    term = a[o:o + CONV_ROWS, :] * cw_ref[k:k + 1, ls]
                    acc = term if acc is None else acc + term
            conv_s[rb * CONV_ROWS:(rb + 1) * CONV_ROWS, ls] = acc + cb_ref[:, ls]
    cbuf[:, 0:CONV_HALO, :] = cbuf[:, tm:tm + CONV_HALO, :]

    a = conv_s[...]
    mu = jnp.mean(a, axis=-1, keepdims=True)
    ac = a - mu
    var = jnp.mean(ac * ac, axis=-1, keepdims=True)
    a = ac * lax.rsqrt(var + EPS) * lng_ref[...] + lnb_ref[...]
    a = a * _sigmoid(a)
    branch_a = _dot(a.astype(BF16), wco_ref[...])

    pos1 = seq_tile * tm + lax.broadcasted_iota(I32, (tm, 1), 0) + 1
    merged = []
    for g, w in enumerate(POOL_WINDOWS):
        cnt = jnp.minimum(pos1, w).astype(F32)
        halves = []
        for s in range(g * POOL_GROUP_IN // LANES, (g + 1) * POOL_GROUP_IN // LANES):
            cur = pbuf[s, pl.ds(POOL_HALO, tm), :]
            acc = cur
            for d in range(1, w):
                acc = acc + pbuf[s, pl.ds(POOL_HALO - d, tm), :]
            halves.append(acc / cnt - cur)
        pooled = jnp.concatenate(halves, axis=-1)
        os_ = slice(g * POOL_GROUP_OUT, (g + 1) * POOL_GROUP_OUT)
        yb = _dot(pooled.astype(BF16), pw_ref[g]) * ps_ref[:, os_]
        ga = z_ref[:, POOL_WIDTH + g * POOL_GROUP_OUT:POOL_WIDTH + (g + 1) * POOL_GROUP_OUT]
        gb = z_ref[:, POOL_WIDTH + D_MODEL + g * POOL_GROUP_OUT:
                   POOL_WIDTH + D_MODEL + (g + 1) * POOL_GROUP_OUT]
        m = ga.astype(F32) * branch_a[:, os_] + gb.astype(F32) * yb
        merged.append(m.astype(BF16))
    pbuf[:, 0:POOL_HALO, :] = pbuf[:, tm:tm + POOL_HALO, :]
    merged = jnp.concatenate(merged, axis=-1)

    x1 = x_ref[...] + _dot(merged, wo_ref[...])
    x1_ref[...] = x1

    v = _rms(x1, ng_ref[...])
    _to_chunked(v_ref, v)
    _zero_pads(v_ref, tm)
    v_hi = v.astype(BF16)
    v_lo = (v - v_hi.astype(F32)).astype(BF16)
    logits = (_dot(v_hi, rwh_ref[...]) + _dot(v_lo, rwh_ref[...]) + _dot(v_hi, rwl_ref[...])
              + rb_ref[...])
    lane = lax.broadcasted_iota(I32, (tm, ROUTER_LANES), 1)
    lane_f = lane.astype(F32)
    neg = jnp.float32(-jnp.inf)
    big = jnp.float32(1e9)

    is_group = jnp.logical_and(lane >= N_EXPERTS, lane < N_EXPERTS + N_GROUPS)
    gl = jnp.where(is_group, logits, neg)
    gmax = jnp.max(gl, axis=-1, keepdims=True)
    gsum = jnp.sum(jnp.exp(gl - gmax), axis=-1, keepdims=True)
    gidx = jnp.min(jnp.where(gl == gmax, lane_f, big), axis=-1, keepdims=True) - N_EXPERTS
    g_val = 1.0 / gsum

    lane_group = jnp.right_shift(lane, 3).astype(F32)
    in_group = jnp.logical_and(lane < N_EXPERTS, lane_group == gidx)
    el = jnp.where(in_group, logits, neg)
    m1 = jnp.max(el, axis=-1, keepdims=True)
    i1 = jnp.min(jnp.where(el == m1, lane_f, big), axis=-1, keepdims=True)
    el2 = jnp.where(lane_f == i1, neg, el)
    m2 = jnp.max(el2, axis=-1, keepdims=True)
    i2 = jnp.min(jnp.where(el2 == m2, lane_f, big), axis=-1, keepdims=True)
    t = jnp.exp(m2 - m1)
    e1 = 1.0 / (1.0 + t)
    e2 = t / (1.0 + t)
    eid_ref[...] = jnp.where(lane == 0, i1, jnp.where(lane == 1, i2, 0.0)).astype(I32)
    rwt_ref[...] = jnp.where(lane == 0, g_val * e1, jnp.where(lane == 1, g_val * e2, 0.0))

    @pl.when(pl.program_id(0) == 0)
    def _():
        cnt_ref[...] = jnp.zeros(cnt_ref.shape, F32)

    picked = jnp.where(jnp.logical_or(lane_f == i1, lane_f == i2), 1.0, 0.0)
    cnt_ref[0:1, :] += jnp.sum(picked, axis=0, keepdims=True)


def _mix_call(x, glu, z, cw, cb, lng, lnb, wco, pw, ps, wo, ng, rwh, rwl, rb, seq_len):
    n = x.shape[0]
    tm = TM_MIX
    zw = z.shape[1]
    const = lambda *shape: pl.BlockSpec(shape, lambda i: (0,) * len(shape))
    row = lambda width: pl.BlockSpec((tm, width), lambda i: (i, 0))
    return pl.pallas_call(
        functools.partial(_mix_kernel, seq_len // tm),
        grid=(n // tm,),
        in_specs=[row(D_MODEL), row(CONV_WIDTH), row(zw),
                  const(CONV_KERNEL, CONV_WIDTH), const(1, CONV_WIDTH), const(1, CONV_WIDTH),
                  const(1, CONV_WIDTH), const(CONV_WIDTH, D_MODEL),
                  const(len(POOL_WINDOWS), POOL_GROUP_IN, POOL_GROUP_OUT), const(1, D_MODEL),
                  const(D_MODEL, D_MODEL), const(1, D_MODEL),
                  const(D_MODEL, ROUTER_LANES), const(D_MODEL, ROUTER_LANES), const(1, ROUTER_LANES)],
        out_specs=[row(D_MODEL), pl.BlockSpec((tm * ROW_PITCH, LANES), lambda i: (i, 0)),
                   row(ROUTER_LANES), row(ROUTER_LANES), const(SUBLANES, ROUTER_LANES)],
        out_shape=[jax.ShapeDtypeStruct((n, D_MODEL), F32),
                   jax.ShapeDtypeStruct((n * ROW_PITCH, LANES), F32),
                   jax.ShapeDtypeStruct((n, ROUTER_LANES), I32),
                   jax.ShapeDtypeStruct((n, ROUTER_LANES), F32),
                   jax.ShapeDtypeStruct((SUBLANES, ROUTER_LANES), F32)],
        scratch_shapes=[pltpu.VMEM((CONV_WIDTH // LANES, CONV_HALO + tm, LANES), F32),
                        pltpu.VMEM((POOL_WIDTH // LANES, POOL_HALO + tm, LANES), F32),
                        pltpu.VMEM((tm, CONV_WIDTH), F32)],
        compiler_params=_params(("arbitrary",)),
        name="mix",
    )(x, glu, z, cw, cb, lng, lnb, wco, pw, ps, wo, ng, rwh, rwl, rb)


def _plan_kernel(eid_ref, offs_ref, dest_ref, carry):
    tp = TM_PLAN

    @pl.when(pl.program_id(0) == 0)
    def _():
        carry[...] = jnp.zeros(carry.shape, F32)

    eid = eid_ref[...]
    lane = lax.broadcasted_iota(I32, (tp, ROUTER_LANES), 1)
    hit0 = lane == eid[:, 0:1]
    hit1 = lane == eid[:, 1:2]
    onehot = jnp.where(jnp.logical_or(hit0, hit1), 1.0, 0.0)
    r = lax.broadcasted_iota(I32, (tp, tp), 0)
    c = lax.broadcasted_iota(I32, (tp, tp), 1)
    tri = jnp.where(c < r, 1.0, 0.0).astype(BF16)
    before = _dot(tri, onehot.astype(BF16)) + carry[0:1, :] + offs_ref[...]
    d0 = jnp.sum(jnp.where(hit0, before, 0.0), axis=-1, keepdims=True)
    d1 = jnp.sum(jnp.where(hit1, before, 0.0), axis=-1, keepdims=True)
    dest_ref[...] = jnp.where(lane == 0, d0, jnp.where(lane == 1, d1, 0.0)).astype(I32)
    carry[0:1, :] += jnp.sum(onehot, axis=0, keepdims=True)


def _plan_call(eid, offs):
    n = eid.shape[0]
    tp = TM_PLAN
    return pl.pallas_call(
        _plan_kernel,
        grid=(n // tp,),
        in_specs=[pl.BlockSpec((tp, ROUTER_LANES), lambda i: (i, 0)),
                  pl.BlockSpec((1, ROUTER_LANES), lambda i: (0, 0))],
        out_specs=pl.BlockSpec((tp, ROUTER_LANES), lambda i: (i, 0)),
        out_shape=jax.ShapeDtypeStruct((n, ROUTER_LANES), I32),
        scratch_shapes=[pltpu.VMEM((SUBLANES, ROUTER_LANES), F32)],
        compiler_params=_params(("arbitrary",)),
        name="plan",
    )(eid, offs)


def _moe_kernel(layer, n_tok,
                dest_ref, cnt_ref,
                v_hbm, wg_hbm, wu_hbm, wd_hbm, out_hbm,
                xbuf, ybuf, xb_s, stg_g, stg_u, stg_d, wgu_s, wd_s,
                row_src, row_dst, offs_s, nxt_s, first_s, cur_s,
                gsem, ssem, wsem):
    tm = TM_MOE
    j = pl.program_id(0)
    n_tiles = pl.num_programs(0)
    slot = j % 2
    row0 = j * tm

    def gather_row(tile, s, r):
        return pltpu.make_async_copy(v_hbm.at[pl.ds(row_src[tile * tm + r], CHUNKS)],
                                     xbuf.at[s, pl.ds(r * ROW_PITCH, CHUNKS)], gsem.at[s])

    def scatter_row(tile, s, r):
        return pltpu.make_async_copy(ybuf.at[s, pl.ds(r * ROW_PITCH, ROW_PITCH)],
                                     out_hbm.at[pl.ds(row_dst[tile * tm + r], ROW_PITCH)], ssem.at[s])

    def start_rows(make, tile, s):
        def body(r, c):
            make(tile, s, r).start()
            return c
        lax.fori_loop(0, tm, body, 0, unroll=64)

    def wait_gather(s):
        pltpu.make_async_copy(v_hbm.at[pl.ds(0, tm * CHUNKS)], xbuf.at[s, pl.ds(0, tm * CHUNKS)],
                              gsem.at[s]).wait()

    def wait_scatter(s):
        pltpu.make_async_copy(ybuf.at[s], out_hbm.at[pl.ds(0, tm * ROW_PITCH)], ssem.at[s]).wait()

    def weight_copies(e):
        w = layer * N_EXPERTS + e
        return [pltpu.make_async_copy(wg_hbm.at[w], stg_g, wsem.at[0]),
                pltpu.make_async_copy(wu_hbm.at[w], stg_u, wsem.at[1]),
                pltpu.make_async_copy(wd_hbm.at[w], stg_d, wsem.at[2])]

    @pl.when(j == 0)
    def _():
        def off_body(e, acc):
            offs_s[e] = acc
            return acc + cnt_ref[e]
        offs_s[N_EXPERTS] = lax.fori_loop(0, N_EXPERTS, off_body, 0)

        def nxt_body(k, nxt):
            e = N_EXPERTS - 1 - k
            nxt_s[e] = nxt
            return jnp.where(cnt_ref[e] > 0, e, nxt)
        first_expert = lax.fori_loop(0, N_EXPERTS, nxt_body, N_EXPERTS)

        def first_body(t, e):
            e = lax.while_loop(lambda e: offs_s[e + 1] <= t * tm, lambda e: e + 1, e)
            first_s[t] = e
            return e
        lax.fori_loop(0, n_tiles, first_body, 0)

        def inv_body(i, c):
            base = i * (8 * ROW_PITCH)
            for k in range(16):
                r = dest_ref[i * 16 + k]
                row_src[r] = base + (k // 2) * ROW_PITCH
                row_dst[r] = base + ((k % 2) * n_tok + k // 2) * ROW_PITCH
            return c
        lax.fori_loop(0, (2 * n_tok) // 16, inv_body, 0)

        for s in range(2):
            _zero_pads(ybuf.at[s], tm)
        cur_s[0] = -1
        for cp in weight_copies(first_expert):
            cp.start()
        start_rows(gather_row, 0, 0)

    wait_gather(slot)

    @pl.when(j + 1 < n_tiles)
    def _():
        start_rows(gather_row, j + 1, 1 - slot)

    xb_s[...] = _from_chunked(xbuf.at[slot], tm).astype(BF16)
    rows = lax.broadcasted_iota(I32, (tm, 1), 0)

    @pl.when(j >= 2)
    def _():
        wait_scatter(slot)

    def expert_step(e):
        @pl.when(cnt_ref[e] > 0)
        def _():
            @pl.when(cur_s[0] != e)
            def _():
                for cp in weight_copies(e):
                    cp.wait()
                wgu_s[:, 0:D_EXPERT] = stg_g[...].astype(BF16)
                wgu_s[:, D_EXPERT:2 * D_EXPERT] = stg_u[...].astype(BF16)
                wd_s[...] = stg_d[...].astype(BF16)
                cur_s[0] = e
                nxt = nxt_s[e]

                @pl.when(nxt < N_EXPERTS)
                def _():
                    for cp in weight_copies(nxt):
                        cp.start()

            lo = offs_s[e] - row0
            hi = offs_s[e + 1] - row0
            gu = _dot(xb_s[...], wgu_s[...])
            gate = gu[:, 0:D_EXPERT]
            h = gate * _sigmoid(gate) * gu[:, D_EXPERT:2 * D_EXPERT]
            h = jnp.where(jnp.logical_and(rows >= lo, rows < hi), h, 0.0)
            y = _dot(h.astype(BF16), wd_s[...])

            @pl.when(e == first_s[j])
            def _():
                _to_chunked(ybuf.at[slot], y)

            @pl.when(e != first_s[j])
            def _():
                _to_chunked(ybuf.at[slot], _from_chunked(ybuf.at[slot], tm) + y)
        return e + 1

    lax.while_loop(lambda e: offs_s[e] < row0 + tm, expert_step, first_s[j])

    start_rows(scatter_row, j, slot)

    @pl.when(j == n_tiles - 1)
    def _():
        wait_scatter(slot)
        wait_scatter(1 - slot)


def _moe_call(v, wg, wu, wd, layer, dest, counts):
    n = v.shape[0] // ROW_PITCH
    tm = TM_MOE
    n_tiles = (2 * n) // tm
    any_spec = pl.BlockSpec(memory_space=pl.ANY)
    return pl.pallas_call(
        functools.partial(_moe_kernel, layer, n),
        grid_spec=pltpu.PrefetchScalarGridSpec(
            num_scalar_prefetch=2,
            grid=(n_tiles,),
            in_specs=[any_spec, any_spec, any_spec, any_spec],
            out_specs=any_spec,
            scratch_shapes=[pltpu.VMEM((2, tm * ROW_PITCH, LANES), F32),
                            pltpu.VMEM((2, tm * ROW_PITCH, LANES), F32),
                            pltpu.VMEM((tm, D_MODEL), BF16),
                            pltpu.VMEM((D_MODEL, D_EXPERT), F32),
                            pltpu.VMEM((D_MODEL, D_EXPERT), F32),
                            pltpu.VMEM((D_EXPERT, D_MODEL), F32),
                            pltpu.VMEM((D_MODEL, 2 * D_EXPERT), BF16),
                            pltpu.VMEM((D_EXPERT, D_MODEL), BF16),
                            pltpu.SMEM((2 * n,), I32),
                            pltpu.SMEM((2 * n,), I32),
                            pltpu.SMEM((N_EXPERTS + 1,), I32),
                            pltpu.SMEM((N_EXPERTS,), I32),
                            pltpu.SMEM((n_tiles,), I32),
                            pltpu.SMEM((1,), I32),
                            pltpu.SemaphoreType.DMA((2,)), pltpu.SemaphoreType.DMA((2,)),
                            pltpu.SemaphoreType.DMA((3,))],
        ),
        out_shape=jax.ShapeDtypeStruct((2 * n * ROW_PITCH, LANES), F32),
        compiler_params=_params(("arbitrary",)),
        name="moe",
    )(dest, counts, v, wg, wu, wd)


def _ple_kernel(last, x1_ref, ya_ref, yb_ref, rwt_ref, p_ref, ng_ref, wg_ref, wp_ref, gn_ref, *out_refs):
    tm = TM_PLE
    rwt = rwt_ref[...]
    x2 = (x1_ref[...] + rwt[:, 0:1] * _from_chunked(ya_ref, tm) + rwt[:, 1:2] * _from_chunked(yb_ref, tm))
    h = _rms(x2, ng_ref[...]).astype(BF16)
    gate = _sigmoid(_dot(h, wg_ref[...]))
    ple = _dot(p_ref[...].astype(BF16), wp_ref[...])
    x3 = x2 + gate * ple
    if last:
        out_refs[0][...] = _rms(x3, gn_ref[...])
    else:
        out_refs[0][...] = x3
        out_refs[1][...] = _rms(x3, gn_ref[...]).astype(BF16)


def _ple_call(x1, y2, rwt, p_all, layer, ng, wg, wp, gn, last):
    n = x1.shape[0]
    tm = TM_PLE
    ple_dim = p_all.shape[-1]
    const = lambda *shape: pl.BlockSpec(shape, lambda i: (0,) * len(shape))
    row = lambda width: pl.BlockSpec((tm, width), lambda i: (i, 0))
    if last:
        out_specs = [row(D_MODEL)]
        out_shape = [jax.ShapeDtypeStruct((n, D_MODEL), F32)]
    else:
        out_specs = [row(D_MODEL), row(D_MODEL)]
        out_shape = [jax.ShapeDtypeStruct((n, D_MODEL), F32), jax.ShapeDtypeStruct((n, D_MODEL), BF16)]
    return pl.pallas_call(
        functools.partial(_ple_kernel, last),
        grid=(n // tm,),
        in_specs=[row(D_MODEL),
                  pl.BlockSpec((tm * ROW_PITCH, LANES), lambda i: (i, 0)),
                  pl.BlockSpec((tm * ROW_PITCH, LANES), lambda i: (i + n // tm, 0)),
                  row(ROUTER_LANES), pl.BlockSpec((None, tm, ple_dim), lambda i: (layer, i, 0)),
                  const(1, D_MODEL), const(D_MODEL, D_MODEL),
                  const(ple_dim, D_MODEL), const(1, D_MODEL)],
        out_specs=out_specs,
        out_shape=out_shape,
        compiler_params=_params(("arbitrary",)),
        name="ple",
    )(x1, y2, y2, rwt, p_all, ng, wg, wp, gn)


def kernel(x, p, norm_mix_g, w_in, b_glu, conv_w, conv_b, conv_ln_g, conv_ln_b, w_conv_out, pool_w, pool_scale, w_out, norm_ffn_g, router_group_w, router_group_b, router_expert_w, router_expert_b, expert_w_gate, expert_w_up, expert_w_down, norm_ple_g, ple_gate_w, ple_proj_w, final_norm_g):
    batch, seq, d = x.shape
    depth = w_in.shape[0]
    n = batch * seq
    assert d == D_MODEL and seq % TM_MIX == 0 and n % TM_IN == 0 and (2 * n) % TM_MOE == 0

    xf = x.reshape(n, d)
    pf = p.reshape(depth, n, p.shape[-1])
    row2 = lambda a: a.reshape(1, -1)
    b_glu3 = b_glu.reshape(depth, 1, -1)
    wg_all = expert_w_gate.reshape(depth * N_EXPERTS, D_MODEL, D_EXPERT)
    wu_all = expert_w_up.reshape(depth * N_EXPERTS, D_MODEL, D_EXPERT)
    wd_all = expert_w_down.reshape(depth * N_EXPERTS, D_EXPERT, D_MODEL)

    u = _norm_call(xf, row2(norm_mix_g[0]))
    out = None
    for l in range(depth):
        glu = _glu_call(u, w_in, b_glu3, l)
        z = _z_call(u, w_in, l)

        pad = ROUTER_LANES - N_EXPERTS - N_GROUPS
        rw = jnp.concatenate([router_expert_w[l], router_group_w[l], jnp.zeros((d, pad), F32)], axis=1)
        rb = jnp.concatenate([router_expert_b[l], router_group_b[l], jnp.zeros((pad,), F32)])
        rw_hi = rw.astype(BF16)
        rw_lo = (rw - rw_hi.astype(F32)).astype(BF16)
        x1, v, eid, rwt, cnt = _mix_call(
            xf, glu, z, conv_w[l], row2(conv_b[l]), row2(conv_ln_g[l]), row2(conv_ln_b[l]),
            w_conv_out[l].astype(BF16), pool_w[l].astype(BF16), row2(pool_scale[l]),
            w_out[l].astype(BF16), row2(norm_ffn_g[l]), rw_hi, rw_lo, row2(rb), seq)

        cnt_row = cnt[0:1, :]
        offs = jnp.cumsum(cnt_row, axis=1) - cnt_row
        dest = _plan_call(eid, offs)[:, :2].reshape(-1)
        y2 = _moe_call(v, wg_all, wu_all, wd_all, l, dest, cnt_row[0, :N_EXPERTS].astype(I32))

        last = l == depth - 1
        gn = final_norm_g if last else norm_mix_g[l + 1]
        res = _ple_call(x1, y2, rwt, pf, l, row2(norm_ple_g[l]), ple_gate_w[l].astype(BF16),
                        ple_proj_w[l].astype(BF16), row2(gn), last)
        if last:
            out = res[0]
        else:
            xf, u = res
    return out.reshape(batch, seq, d)
```
